```python
import math
import jax, jax.numpy as jnp
from jax import lax
import numpy as np

D_MODEL = 1024
BATCH = 4
SEQ = 4096
DEPTH = 2
DEC_BATCH = 8
DEC_SEQ = 16
PAST_LEN = 2048

CHUNK = 64
N_META = 16
N_A = DEPTH // 2
N_B = DEPTH - N_A
RW_HEAD = 64
RW_HEADS = D_MODEL // RW_HEAD
LORA_W = 64
LORA_A = 64
LORA_G = 160
GN_EPS = 64e-5
HEAD_DIM = 64
N_HEADS = D_MODEL // (2 * HEAD_DIM)
V_DIM = 2 * HEAD_DIM
ROPE_DIM = HEAD_DIM // 4
ROPE_THETA = 500000.0
SUBLN_EPS = 1e-5
Q_BLOCK = 128
NEG_INF = -1e30
D_FF = 2816
NORM_EPS = 1e-6

kernel_name = 'yoco_rwkv7_diffattn_streaming_step'


def rmsnorm(x, g, eps=NORM_EPS):
    xf = x.astype(jnp.float32)
    y = xf * lax.rsqrt(jnp.mean(xf * xf, axis=-1, keepdims=True) + eps)
    return (y * g.astype(jnp.float32)).astype(x.dtype)


def swiglu(x, w_gate, w_up, w_down):
    return (jax.nn.silu(x @ w_gate) * (x @ w_up)) @ w_down


def rope(x, pos):
    half = ROPE_DIM // 2
    inv = jnp.power(jnp.float32(ROPE_THETA), -jnp.arange(0, ROPE_DIM, 2, dtype=jnp.float32) / ROPE_DIM)
    ang = pos.astype(jnp.float32)[:, None] * inv[None, :]
    cos = jnp.cos(ang)[None, :, None, None, :]
    sin = jnp.sin(ang)[None, :, None, None, :]
    x1 = x[..., :half].astype(jnp.float32)
    x2 = x[..., half:ROPE_DIM].astype(jnp.float32)
    rot = jnp.concatenate([x1 * cos - x2 * sin, x2 * cos + x1 * sin], axis=-1)
    return jnp.concatenate([rot.astype(x.dtype), x[..., ROPE_DIM:]], axis=-1)


def _wkv7_step(S, inp):
    r_t, w_t, k_t, v_t, a_t, b_t = inp
    sa = jnp.einsum('bhvk,bhk->bhv', S, a_t)
    S = S * w_t[:, :, None, :] + sa[..., None] * b_t[:, :, None, :] + v_t[..., None] * k_t[:, :, None, :]
    y = jnp.einsum('bhvk,bhk->bhv', S, r_t)
    return S, y


def rwkv7_time_mix(x, shift_prev, S0, P, i):
    B, T, C = x.shape
    H, N = RW_HEADS, RW_HEAD
    f32 = jnp.float32
    dt = x.dtype
    x_prev = jnp.concatenate([shift_prev[:, None, :].astype(dt), x[:, :-1]], axis=1)
    xx = x_prev - x
    mu = P['rw_mu'][i]
    xr, xw, xk, xv, xa, xg = (x + xx * mu[m] for m in range(6))
    r = xr @ P['rw_w_r'][i]
    k = xk @ P['rw_w_k'][i]
    v = xv @ P['rw_w_v'][i]
    w_log = -jax.nn.softplus(-(P['rw_w0'][i] + jnp.tanh(xw @ P['rw_w1'][i]) @ P['rw_w2'][i])) - 0.5
    decay = jnp.exp(-jnp.exp(w_log.astype(f32)))
    a = jax.nn.sigmoid(P['rw_a0'][i] + (xa @ P['rw_a1'][i]) @ P['rw_a2'][i])
    g = jax.nn.sigmoid(xg @ P['rw_g1'][i]) @ P['rw_g2'][i]
    kk = (k * P['rw_k_k'][i]).reshape(B, T, H, N).astype(f32)
    kk = kk / jnp.maximum(jnp.linalg.norm(kk, axis=-1, keepdims=True), 1e-12)
    k = k * (1 + (a - 1) * P['rw_k_a'][i])
    heads = lambda z: z.reshape(B, T, H, N).astype(f32)
    r_h, k_h, v_h, a_h = heads(r), heads(k), heads(v), heads(a)
    w_h = decay.reshape(B, T, H, N)
    seq = tuple(jnp.moveaxis(z, 1, 0) for z in (r_h, w_h, k_h, v_h, -kk, kk * a_h))
    S_T, ys = lax.scan(_wkv7_step, S0.astype(f32), seq)
    y = jnp.moveaxis(ys, 0, 1)
    mean = jnp.mean(y, axis=-1, keepdims=True)
    var = jnp.mean(jnp.square(y - mean), axis=-1, keepdims=True)
    y = ((y - mean) * lax.rsqrt(var + GN_EPS)).reshape(B, T, C)
    y = y * P['rw_gn_g'][i].astype(f32) + P['rw_gn_b'][i].astype(f32)
    bonus = jnp.sum(r_h * k_h * P['rw_r_k'][i].astype(f32), axis=-1, keepdims=True) * v_h
    y = (y + bonus.reshape(B, T, C)).astype(dt)
    out = (y * g) @ P['rw_w_o'][i]
    return out, x[:, -1], S_T.astype(S0.dtype)


def diff_softmax_attn(q, k, v, lam, mask):
    s = jnp.einsum('bqhcd,bkhcd->bhcqk', q, k).astype(jnp.float32) * (HEAD_DIM ** -0.5)
    if mask is not None:
        s = jnp.where(mask, s, NEG_INF)
    p = jax.nn.softmax(s, axis=-1)
    pd = p[:, :, 0] - lam * p[:, :, 1]
    return jnp.einsum('bhqk,bkhe->bqhe', pd.astype(v.dtype), v)


def attend_prompt(q, k, v, lam):
    B = q.shape[0]
    n_frames = q.shape[1] - N_META
    o_meta = diff_softmax_attn(q[:, :N_META], k[:, :N_META], v[:, :N_META], lam, None)
    key_idx = jnp.arange(k.shape[1])
    key_cid = jnp.where(key_idx < N_META, -1, (key_idx - N_META) // CHUNK)
    n_blk = n_frames // Q_BLOCK
    qb = jnp.swapaxes(q[:, N_META:].reshape(B, n_blk, Q_BLOCK, N_HEADS, 2, HEAD_DIM), 0, 1)

    def blk(args):
        q_blk, b = args
        q_cid = (b * Q_BLOCK + jnp.arange(Q_BLOCK)) // CHUNK
        mask = key_cid[None, :] <= q_cid[:, None]
        return diff_softmax_attn(q_blk, k, v, lam, mask)

    o_frames = lax.map(blk, (qb, jnp.arange(n_blk)))
    o_frames = jnp.swapaxes(o_frames, 0, 1).reshape(B, n_frames, N_HEADS, V_DIM)
    return jnp.concatenate([o_meta, o_frames], axis=1)


def attend_sample(q, k, v, lam):
    return diff_softmax_attn(q, k, v, lam, None)


def shared_kv(h, pos, P):
    B, T, _ = h.shape
    hn = rmsnorm(h, P['kv_norm'])
    k = rope((hn @ P['w_k']).reshape(B, T, N_HEADS, 2, HEAD_DIM), pos)
    v = (hn @ P['w_v']).reshape(B, T, N_HEADS, V_DIM)
    return k, v


def diff_attn_layer(h, pos, k_all, v_all, attend, P, li):
    j = li - N_A
    B, T, _ = h.shape
    q = rope((h @ P['b_w_q'][j]).reshape(B, T, N_HEADS, 2, HEAD_DIM), pos)
    lam_init = 0.8 - 0.6 * math.exp(-0.3 * li)
    lp = P['b_lambda'][j].astype(jnp.float32)
    lam = jnp.exp(jnp.sum(lp[0] * lp[1])) - jnp.exp(jnp.sum(lp[2] * lp[3])) + lam_init
    o = attend(q, k_all, v_all, lam)
    o = rmsnorm(o, P['b_subln'][j], SUBLN_EPS) * (1.0 - lam_init)
    return o.reshape(B, T, N_HEADS * V_DIM) @ P['b_w_o'][j]


def trunk(x, pos, shift0, wkv0, extend_kv, attend, P):
    shifts, wkvs = [], []
    k_new = v_new = k_all = v_all = None
    for li in range(DEPTH):
        if li == N_A:
            k_new, v_new = shared_kv(x, pos, P)
            k_all, v_all = extend_kv(k_new, v_new)
        x = x + 0.5 * swiglu(rmsnorm(x, P['norm_ffn'][li, 0]), P['ffn_w_gate'][li, 0], P['ffn_w_up'][li, 0], P['ffn_w_down'][li, 0])
        h = rmsnorm(x, P['norm_mix'][li])
        if li < N_A:
            y, s_last, S = rwkv7_time_mix(h, shift0[:, li], wkv0[:, li], P, li)
            shifts.append(s_last)
            wkvs.append(S)
        else:
            y = diff_attn_layer(h, pos, k_all, v_all, attend, P, li)
        x = x + y
        x = x + 0.5 * swiglu(rmsnorm(x, P['norm_ffn'][li, 1]), P['ffn_w_gate'][li, 1], P['ffn_w_up'][li, 1], P['ffn_w_down'][li, 1])
    return x, jnp.stack(shifts, axis=1), jnp.stack(wkvs, axis=1), k_new, v_new


def setup_inputs(seed: int = 0) -> dict:
    key = jax.random.key(seed)
    ks = iter(jax.random.split(key, 64))
    f32 = jnp.float32
    nrm = lambda shape, scale: jax.random.normal(next(ks), shape, f32) * scale
    D = D_MODEL
    HK = N_HEADS * 2 * HEAD_DIM
    HV = N_HEADS * V_DIM
    T_c = N_META + PAST_LEN
    return {
        'x_prompt': nrm((BATCH, SEQ, D), 1.0),
        'x_sample': nrm((DEC_BATCH, DEC_SEQ, D), 1.0),
        'cache_k': nrm((DEC_BATCH, T_c, N_HEADS, 2 * HEAD_DIM), 1.0),
        'cache_v': nrm((DEC_BATCH, T_c, N_HEADS, V_DIM), 1.0),
        'state_wkv': nrm((DEC_BATCH, N_A, RW_HEADS, RW_HEAD, RW_HEAD), 0.1),
        'state_shift': nrm((DEC_BATCH, N_A, D), 1.0),
        'meta_tokens': nrm((N_META, D), 1.0),
        'norm_ffn': 1.0 + nrm((DEPTH, 2, D), 0.02),
        'norm_mix': 1.0 + nrm((DEPTH, D), 0.02),
        'ffn_w_gate': nrm((DEPTH, 2, D, D_FF), D ** -0.5),
        'ffn_w_up': nrm((DEPTH, 2, D, D_FF), D ** -0.5),
        'ffn_w_down': nrm((DEPTH, 2, D_FF, D), D_FF ** -0.5),
        'rw_mu': jax.random.uniform(next(ks), (N_A, 6, D), f32, 0.0, 1.0),
        'rw_w_r': nrm((N_A, D, D), D ** -0.5),
        'rw_w_k': nrm((N_A, D, D), D ** -0.5),
        'rw_w_v': nrm((N_A, D, D), D ** -0.5),
        'rw_w_o': nrm((N_A, D, D), D ** -0.5),
        'rw_w0': -0.5 + nrm((N_A, D), 1.0),
        'rw_w1': nrm((N_A, D, LORA_W), D ** -0.5),
        'rw_w2': nrm((N_A, LORA_W, D), LORA_W ** -0.5),
        'rw_a0': nrm((N_A, D), 0.1),
        'rw_a1': nrm((N_A, D, LORA_A), D ** -0.5),
        'rw_a2': nrm((N_A, LORA_A, D), LORA_A ** -0.5),
        'rw_g1': nrm((N_A, D, LORA_G), D ** -0.5),
        'rw_g2': nrm((N_A, LORA_G, D), LORA_G ** -0.5),
        'rw_k_k': 0.85 + nrm((N_A, D), 0.02),
        'rw_k_a': 1.0 + nrm((N_A, D), 0.02),
        'rw_r_k': nrm((N_A, RW_HEADS, RW_HEAD), 0.1),
        'rw_gn_g': 1.0 + nrm((N_A, D), 0.02),
        'rw_gn_b': nrm((N_A, D), 0.02),
        'kv_norm': 1.0 + nrm((D,), 0.02),
        'w_k': nrm((D, HK), D ** -0.5),
        'w_v': nrm((D, HV), D ** -0.5),
        'b_w_q': nrm((N_B, D, HK), D ** -0.5),
        'b_w_o': nrm((N_B, HV, D), HV ** -0.5),
        'b_lambda': nrm((N_B, 4, HEAD_DIM), 0.1),
        'b_subln': 1.0 + nrm((N_B, V_DIM), 0.02),
        'final_norm': 1.0 + nrm((D,), 0.02),
    }


def reference(x_prompt, x_sample, cache_k, cache_v, state_wkv, state_shift, meta_tokens,
              norm_ffn, norm_mix, ffn_w_gate, ffn_w_up, ffn_w_down,
              rw_mu, rw_w_r, rw_w_k, rw_w_v, rw_w_o, rw_w0, rw_w1, rw_w2,
              rw_a0, rw_a1, rw_a2, rw_g1, rw_g2, rw_k_k, rw_k_a, rw_r_k, rw_gn_g, rw_gn_b,
              kv_norm, w_k, w_v, b_w_q, b_w_o, b_lambda, b_subln, final_norm):
    P = dict(norm_ffn=norm_ffn, norm_mix=norm_mix, ffn_w_gate=ffn_w_gate, ffn_w_up=ffn_w_up,
             ffn_w_down=ffn_w_down, rw_mu=rw_mu, rw_w_r=rw_w_r, rw_w_k=rw_w_k, rw_w_v=rw_w_v,
             rw_w_o=rw_w_o, rw_w0=rw_w0, rw_w1=rw_w1, rw_w2=rw_w2, rw_a0=rw_a0, rw_a1=rw_a1,
             rw_a2=rw_a2, rw_g1=rw_g1, rw_g2=rw_g2, rw_k_k=rw_k_k, rw_k_a=rw_k_a, rw_r_k=rw_r_k,
             rw_gn_g=rw_gn_g, rw_gn_b=rw_gn_b, kv_norm=kv_norm, w_k=w_k, w_v=w_v, b_w_q=b_w_q,
             b_w_o=b_w_o, b_lambda=b_lambda, b_subln=b_subln)

    Bp, Tp = x_prompt.shape[0], x_prompt.shape[1]
    meta = jnp.broadcast_to(meta_tokens[None].astype(x_prompt.dtype), (Bp, N_META, D_MODEL))
    xp = jnp.concatenate([meta, x_prompt], axis=1)
    pos_p = jnp.arange(N_META + Tp)
    shift0_p = jnp.zeros((Bp, N_A, D_MODEL), x_prompt.dtype)
    wkv0_p = jnp.zeros((Bp, N_A, RW_HEADS, RW_HEAD, RW_HEAD), state_wkv.dtype)
    hp, shift_p, wkv_p, k_p, v_p = trunk(xp, pos_p, shift0_p, wkv0_p, lambda kn, vn: (kn, vn), attend_prompt, P)
    y_prompt = rmsnorm(hp[:, N_META:], final_norm)

    Bd, Ts = x_sample.shape[0], x_sample.shape[1]
    past = cache_k.shape[1] - N_META
    pos_s = N_META + past + jnp.arange(Ts)

    def extend_sample(kn, vn):
        k_all = jnp.concatenate([cache_k.reshape(Bd, -1, N_HEADS, 2, HEAD_DIM).astype(kn.dtype), kn], axis=1)
        v_all = jnp.concatenate([cache_v.astype(vn.dtype), vn], axis=1)
        return k_all, v_all

    hs, shift_s, wkv_s, k_s, v_s = trunk(x_sample, pos_s, state_shift, state_wkv, extend_sample, attend_sample, P)
    y_sample = rmsnorm(hs, final_norm)

    new_cache_k_p = k_p.reshape(Bp, N_META + Tp, N_HEADS, 2 * HEAD_DIM)
    new_cache_v_p = v_p
    new_cache_k_s = k_s.reshape(Bd, Ts, N_HEADS, 2 * HEAD_DIM)
    new_cache_v_s = v_s
    return (y_prompt, y_sample, wkv_p, shift_p, new_cache_k_p, new_cache_v_p, wkv_s, shift_s, new_cache_k_s, new_cache_v_s)
```

```python
import functools
import math

import jax
import jax.numpy as jnp
from jax import lax
from jax.experimental import pallas as pl
from jax.experimental.pallas import tpu as pltpu

f32 = jnp.float32
bf16 = jnp.bfloat16

D_MODEL = 1024
CHUNK = 64
N_META = 16
RW_HEAD = 64
RW_HEADS = D_MODEL // RW_HEAD
GN_EPS = 64e-5
HEAD_DIM = 64
N_HEADS = D_MODEL // (2 * HEAD_DIM)
V_DIM = 2 * HEAD_DIM
ROPE_DIM = HEAD_DIM // 4
ROPE_THETA = 500000.0
SUBLN_EPS = 1e-5
NEG_INF = -1e30
NORM_EPS = 1e-6
LAM_INIT = 0.8 - 0.6 * math.exp(-0.3 * 1)

LANES = 128
HEADS_PER_VREG = LANES // RW_HEAD
N_PAIRS = D_MODEL // LANES
VMEM_LIMIT = 56 * 1024 * 1024


def _cparams(sem):
    return pltpu.CompilerParams(dimension_semantics=sem, vmem_limit_bytes=VMEM_LIMIT)


def _dot(a, b):
    return jnp.dot(a, b, preferred_element_type=f32)


def _dot_nt(a, b):
    return lax.dot_general(a, b, (((1,), (1,)), ((), ())), preferred_element_type=f32)


def _dot_tn(a, b):
    return lax.dot_general(a, b, (((0,), (0,)), ((), ())), preferred_element_type=f32)


def _split2(x):
    hi = x.astype(bf16)
    lo = (x - hi.astype(f32)).astype(bf16)
    return hi, lo


def _split3(x):
    h1 = x.astype(bf16)
    r1 = x - h1.astype(f32)
    h2 = r1.astype(bf16)
    h3 = (r1 - h2.astype(f32)).astype(bf16)
    return h1, h2, h3


def _sigmoid(x):
    return 1.0 / (1.0 + jnp.exp(-x))


def _rms(x, g, eps):
    return x * lax.rsqrt(jnp.mean(x * x, axis=-1, keepdims=True) + eps) * g


def _headsum(x, g1, g2):
    hi, lo = _split2(x)
    s = _dot(hi, g1) + _dot(lo, g1)
    shi, slo = _split2(s)
    return _dot(shi, g2) + _dot(slo, g2)


def _rope(x, cos, sin_a, sin_b):
    reps = x.shape[1] // LANES
    c = jnp.concatenate([cos] * reps, axis=1)
    sa = jnp.concatenate([sin_a] * reps, axis=1)
    sb = jnp.concatenate([sin_b] * reps, axis=1)
    half = ROPE_DIM // 2
    return x * c + pltpu.roll(x, x.shape[1] - half, 1) * sa + pltpu.roll(x, half, 1) * sb


def _row(i):
    return (i, 0)


def _fixed2(i):
    return (0, 0)


def _rows(tm, w):
    return pl.BlockSpec((tm, w), _row)


def _whole(shape):
    return pl.BlockSpec(shape, _fixed2, pipeline_mode=pl.Buffered(1))


def _ffn_kernel(x_ref, g_ref, wg_ref, wu_ref, wd_ref, *rest, final_norm):
    o_ref = rest[-1]
    x = x_ref[...]
    xb = _rms(x, g_ref[...], NORM_EPS).astype(bf16)
    gate = _dot(xb, wg_ref[...])
    up = _dot(xb, wu_ref[...])
    act = (gate * _sigmoid(gate) * up).astype(bf16)
    y = x + 0.5 * _dot(act, wd_ref[...])
    if final_norm:
        y = _rms(y, rest[0][...], NORM_EPS)
    o_ref[...] = y


def _ffn(x, g, wg, wu, wd, tm, final_g=None):
    n, d = x.shape
    dff = wg.shape[1]
    ins = [x, g, wg, wu, wd]
    specs = [_rows(tm, d), _whole((1, d)), _whole((d, dff)), _whole((d, dff)), _whole((dff, d))]
    if final_g is not None:
        ins.append(final_g)
        specs.append(_whole((1, d)))
    return pl.pallas_call(
        functools.partial(_ffn_kernel, final_norm=final_g is not None),
        grid=(n // tm,),
        in_specs=specs,
        out_specs=_rows(tm, d),
        out_shape=jax.ShapeDtypeStruct((n, d), f32),
        compiler_params=_cparams(("parallel",)),
        name="ffn_half",
    )(*ins)


def _rwkv_proj_kernel(x_ref, xprev_ref, shift_ref, nm_ref, mu_ref, wr_ref, wk_ref, wv_ref,
                      w0_ref, w1_ref, w2_ref, a0_ref, a1_ref, a2_ref, g1_ref, g2_ref,
                      kk_ref, ka_ref, hs1_ref, hs2_ref,
                      r_ref, lw_ref, k_ref, v_ref, as_ref, bs_ref, g_ref, hlast_ref,
                      h_scr, *, tm, tiles_per_seq):
    nm = nm_ref[...]
    h = _rms(x_ref[...], nm, NORM_EPS)
    hlast_ref[0] = h[tm - 1:tm, :]
    first = (pl.program_id(0) % tiles_per_seq) == 0
    h_before = _rms(xprev_ref[7:8, :], nm, NORM_EPS)
    h_scr[8:tm + 8, :] = h
    h_scr[7:8, :] = jnp.where(first, shift_ref[0], h_before)
    xx = h_scr[7:tm + 7, :] - h
    mu = mu_ref[...]

    def mix(m):
        return (h + xx * mu[m:m + 1, :]).astype(bf16)

    r = _dot(mix(0), wr_ref[...])
    k = _dot(mix(2), wk_ref[...])
    v = _dot(mix(3), wv_ref[...])
    ww = w0_ref[...] + _dot(jnp.tanh(_dot(mix(1), w1_ref[...])).astype(bf16), w2_ref[...])
    neg = -ww
    softplus = jnp.maximum(neg, 0.0) + jnp.log(1.0 + jnp.exp(-jnp.abs(neg)))
    lw = -jnp.exp(-softplus - 0.5)
    a = _sigmoid(a0_ref[...] + _dot(_dot(mix(4), a1_ref[...]).astype(bf16), a2_ref[...]))
    g = _dot(_sigmoid(_dot(mix(5), g1_ref[...])).astype(bf16), g2_ref[...])
    kk = k * kk_ref[...]
    norm = jnp.sqrt(_headsum(kk * kk, hs1_ref[...], hs2_ref[...]))
    kk = kk / jnp.maximum(norm, 1e-12)
    r_ref[...] = r
    lw_ref[...] = lw
    k_ref[...] = k * (1.0 + (a - 1.0) * ka_ref[...])
    v_ref[...] = v
    as_ref[...] = -kk
    bs_ref[...] = kk * a
    g_ref[...] = g


def _rwkv_proj(x, shift0, p, tm, tiles_per_seq):
    n, d = x.shape
    nt = n // tm
    row = _rows(tm, d)
    lw_, la_, lg_ = p["w1"].shape[1], p["a1"].shape[1], p["g1"].shape[1]
    specs = [
        row,
        pl.BlockSpec((8, d), lambda i: (jnp.maximum(i * (tm // 8) - 1, 0), 0)),
        pl.BlockSpec((1, 1, d), lambda i: (i // tiles_per_seq, 0, 0)),
        _whole((1, d)), _whole((6, d)),
        _whole((d, d)), _whole((d, d)), _whole((d, d)),
        _whole((1, d)), _whole((d, lw_)), _whole((lw_, d)),
        _whole((1, d)), _whole((d, la_)), _whole((la_, d)),
        _whole((d, lg_)), _whole((lg_, d)),
        _whole((1, d)), _whole((1, d)), _whole((d, LANES)), _whole((LANES, d)),
    ]
    outs = [jax.ShapeDtypeStruct((n, d), f32)] * 7 + [jax.ShapeDtypeStruct((nt, 1, d), f32)]
    out_specs = [row] * 7 + [pl.BlockSpec((1, 1, d), lambda i: (i, 0, 0))]
    return pl.pallas_call(
        functools.partial(_rwkv_proj_kernel, tm=tm, tiles_per_seq=tiles_per_seq),
        grid=(nt,),
        in_specs=specs,
        out_specs=out_specs,
        out_shape=outs,
        scratch_shapes=[pltpu.VMEM((tm + 8, d), f32)],
        compiler_params=_cparams(("parallel",)),
        name="rwkv_proj",
    )(x, x, shift0, p["norm_mix"], p["mu"], p["w_r"], p["w_k"], p["w_v"],
      p["w0"], p["w1"], p["w2"], p["a0"], p["a1"], p["a2"], p["g1"], p["g2"],
      p["k_k"], p["k_a"], p["hs1"], p["hs2"])


def _wkv_kernel(r_ref, lw_ref, k_ref, v_ref, a_ref, b_ref, s0_ref, y_ref, sT_ref, s_scr, *, L):
    c = pl.program_id(1)

    @pl.when(c == 0)
    def _():
        s_scr[...] = s0_ref[0]

    L2 = 2 * L
    ri = lax.broadcasted_iota(jnp.int32, (L, L), 0)
    ci = lax.broadcasted_iota(jnp.int32, (L, L), 1)
    cum = jnp.where(ri >= ci, 1.0, 0.0).astype(bf16)
    rs = lax.broadcasted_iota(jnp.int32, (L2, LANES), 0)
    ls = lax.broadcasted_iota(jnp.int32, (L2, LANES), 1)
    own = (rs < L) == (ls < RW_HEAD)
    rg = lax.broadcasted_iota(jnp.int32, (L2, L2), 0) & (L - 1)
    cg = lax.broadcasted_iota(jnp.int32, (L2, L2), 1) & (L - 1)
    strict = rg > cg
    incl = rg >= cg

    def stack(x):
        return jnp.where(own, jnp.concatenate([x, x], axis=0), 0.0).astype(bf16)

    for p in range(N_PAIRS):
        cols = slice(p * LANES, (p + 1) * LANES)
        lw = lw_ref[0, :, cols]
        l1, l2, l3 = _split3(lw)
        cs = _dot(cum, l1) + _dot(cum, l2) + _dot(cum, l3)
        w_in = jnp.exp(cs)
        w_inv = jnp.exp(-cs)
        a_st = stack(a_ref[0, :, cols] * jnp.exp(cs - lw))
        r_st = stack(r_ref[0, :, cols] * w_in)
        b_st = stack(b_ref[0, :, cols] * w_inv)
        k_st = stack(k_ref[0, :, cols] * w_inv)
        v_st = stack(v_ref[0, :, cols])
        gram = _dot_nt(jnp.concatenate([a_st, r_st], axis=0), jnp.concatenate([b_st, k_st], axis=0))
        n_ab = jnp.where(strict, gram[:L2, :L2], 0.0)
        a_ak = jnp.where(strict, gram[:L2, L2:], 0.0).astype(bf16)
        a_rb = jnp.where(incl, gram[L2:, :L2], 0.0).astype(bf16)
        a_rk = jnp.where(incl, gram[L2:, L2:], 0.0).astype(bf16)
        m = n_ab
        t = n_ab
        for _ in range(L.bit_length() - 2):
            mb = m.astype(bf16)
            m = _dot(mb, mb)
            t = t + m + _dot(t.astype(bf16), m.astype(bf16))
        s = s_scr[p]
        sb = s.astype(bf16)
        rhs = _dot_nt(a_st, sb) + _dot(a_ak, v_st)
        u = rhs + _dot(t.astype(bf16), rhs.astype(bf16))
        ub = u.astype(bf16)
        y_st = _dot_nt(r_st, sb) + _dot(a_rb, ub) + _dot(a_rk, v_st)
        y_ref[0, :, cols] = y_st[:L] + y_st[L:]
        s_scr[p] = (s + _dot_tn(ub, b_st) + _dot_tn(v_st, k_st)) * w_in[L - 1:L, :]

    @pl.when(c == pl.num_programs(1) - 1)
    def _():
        sT_ref[0] = s_scr[...]


def _wkv(r, lw, k, v, a, b, s0, L):
    nseq, t, d = r.shape
    blk = pl.BlockSpec((1, L, d), lambda s, c: (s, c, 0))
    st = pl.BlockSpec((1, N_PAIRS, LANES, LANES), lambda s, c: (s, 0, 0, 0))
    return pl.pallas_call(
        functools.partial(_wkv_kernel, L=L),
        grid=(nseq, t // L),
        in_specs=[blk] * 6 + [st],
        out_specs=[blk, st],
        out_shape=[jax.ShapeDtypeStruct((nseq, t, d), f32),
                   jax.ShapeDtypeStruct((nseq, N_PAIRS, LANES, LANES), f32)],
        scratch_shapes=[pltpu.VMEM((N_PAIRS, LANES, LANES), f32)],
        compiler_params=_cparams(("parallel", "arbitrary")),
        name="wkv_scan",
    )(r, lw, k, v, a, b, s0)


def _pair_states(s):
    n = s.shape[0]
    s = s.reshape(n, N_PAIRS, HEADS_PER_VREG, RW_HEAD, RW_HEAD)
    eye = jnp.eye(HEADS_PER_VREG, dtype=s.dtype)
    return jnp.einsum("npivk,ij->npivjk", s, eye).reshape(n, N_PAIRS, LANES, LANES)


def _head_states(s):
    n = s.shape[0]
    s = s.reshape(n, N_PAIRS, HEADS_PER_VREG, RW_HEAD, HEADS_PER_VREG, RW_HEAD)
    s = jnp.stack([s[:, :, i, :, i, :] for i in range(HEADS_PER_VREG)], axis=2)
    return s.reshape(n, RW_HEADS, RW_HEAD, RW_HEAD)


def _rwkv_out_kernel(x_ref, y_ref, r_ref, k_ref, v_ref, g_ref, rk_ref, gg_ref, gb_ref,
                     wo_ref, hs1_ref, hs2_ref, o_ref):
    hs1 = hs1_ref[...]
    hs2 = hs2_ref[...]
    y = y_ref[...]
    d = y - _headsum(y, hs1, hs2) * (1.0 / RW_HEAD)
    var = _headsum(d * d, hs1, hs2) * (1.0 / RW_HEAD)
    yn = d * lax.rsqrt(var + GN_EPS) * gg_ref[...] + gb_ref[...]
    v = v_ref[...]
    bonus = _headsum(r_ref[...] * k_ref[...] * rk_ref[...], hs1, hs2) * v
    o_ref[...] = x_ref[...] + _dot(((yn + bonus) * g_ref[...]).astype(bf16), wo_ref[...])


def _rwkv_out(x, y, r, k, v, g, p, tm):
    n, d = x.shape
    row = _rows(tm, d)
    return pl.pallas_call(
        _rwkv_out_kernel,
        grid=(n // tm,),
        in_specs=[row] * 6 + [_whole((1, d))] * 3 + [_whole((d, d)), _whole((d, LANES)), _whole((LANES, d))],
        out_specs=row,
        out_shape=jax.ShapeDtypeStruct((n, d), f32),
        compiler_params=_cparams(("parallel",)),
        name="rwkv_out",
    )(x, y, r, k, v, g, p["r_k"], p["gn_g"], p["gn_b"], p["w_o"], p["hs1"], p["hs2"])


def _kv_proj_kernel(x_ref, g_ref, wk_ref, wv_ref, cos_ref, sa_ref, sb_ref, k_ref, v_ref, kb_ref, vb_ref):
    hb = _rms(x_ref[...], g_ref[...], NORM_EPS).astype(bf16)
    k = _rope(_dot(hb, wk_ref[...]), cos_ref[...], sa_ref[...], sb_ref[...])
    v = _dot(hb, wv_ref[...])
    k_ref[...] = k
    v_ref[...] = v
    kb_ref[...] = k.astype(bf16)
    vb_ref[...] = v.astype(bf16)


def _kv_proj(x, g, wk, wv, rope_tabs, tm, tab_blocks):
    n, d = x.shape
    row = _rows(tm, d)
    tab = pl.BlockSpec((tm, LANES), lambda i: (i % tab_blocks, 0))
    return pl.pallas_call(
        _kv_proj_kernel,
        grid=(n // tm,),
        in_specs=[row, _whole((1, d)), _whole((d, d)), _whole((d, d)), tab, tab, tab],
        out_specs=[row] * 4,
        out_shape=[jax.ShapeDtypeStruct((n, d), f32)] * 2 + [jax.ShapeDtypeStruct((n, d), bf16)] * 2,
        compiler_params=_cparams(("parallel",)),
        name="kv_proj",
    )(x, g, wk, wv, *rope_tabs)


def _q_proj_kernel(x_ref, g_ref, wq_ref, cos_ref, sa_ref, sb_ref, q_ref):
    hb = _rms(x_ref[...], g_ref[...], NORM_EPS).astype(bf16)
    q = _rope(_dot(hb, wq_ref[...]), cos_ref[...], sa_ref[...], sb_ref[...])
    q_ref[...] = (q * HEAD_DIM ** -0.5).astype(bf16)


def _q_proj(x, g, wq, rope_tabs, tm, tab_blocks):
    n, d = x.shape
    row = _rows(tm, d)
    tab = pl.BlockSpec((tm, LANES), lambda i: (i % tab_blocks, 0))
    return pl.pallas_call(
        _q_proj_kernel,
        grid=(n // tm,),
        in_specs=[row, _whole((1, d)), _whole((d, d)), tab, tab, tab],
        out_specs=row,
        out_shape=jax.ShapeDtypeStruct((n, d), bf16),
        compiler_params=_cparams(("parallel",)),
        name="q_proj",
    )(x, g, wq, *rope_tabs)


def _lam(lp):
    e1 = jnp.exp(jnp.sum(lp[0:1, :] * lp[1:2, :], axis=-1, keepdims=True))
    e2 = jnp.exp(jnp.sum(lp[2:3, :] * lp[3:4, :], axis=-1, keepdims=True))
    return e1 - e2 + LAM_INIT


def _split_heads(q):
    lane = lax.broadcasted_iota(jnp.int32, q.shape, 1)
    zero = jnp.zeros_like(q)
    return jnp.where(lane < HEAD_DIM, q, zero), jnp.where(lane >= HEAD_DIM, q, zero)


class _Softmax:
    def __init__(self, m, l, acc):
        self.m, self.l, self.acc = m, l, acc

    @staticmethod
    def start(s, v):
        m = jnp.max(s, axis=-1, keepdims=True)
        p = jnp.exp(s - m)
        return _Softmax(m, jnp.sum(p, axis=-1, keepdims=True), _dot(p.astype(bf16), v))

    def add(self, s, v):
        m = jnp.maximum(self.m, jnp.max(s, axis=-1, keepdims=True))
        alpha = jnp.exp(self.m - m)
        p = jnp.exp(s - m)
        return _Softmax(m, alpha * self.l + jnp.sum(p, axis=-1, keepdims=True),
                        alpha * self.acc + _dot(p.astype(bf16), v))

    def tup(self):
        return (self.m, self.l, self.acc)


def _attn_finish(s1, s2, lp_ref, sub_ref, o_ref):
    o = s1.acc / s1.l - _lam(lp_ref[...]) * (s2.acc / s2.l)
    o = o * lax.rsqrt(jnp.mean(o * o, axis=-1, keepdims=True) + SUBLN_EPS) * sub_ref[...]
    o_ref[0] = (o * (1.0 - LAM_INIT)).astype(o_ref.dtype)


def _attn_prompt_kernel(q_ref, k_ref, v_ref, km_ref, vm_ref, lp_ref, sub_ref, o_ref, *, tq):
    i = pl.program_id(2)
    q1, q2 = _split_heads(q_ref[0])
    km = km_ref[...]
    vm = vm_ref[...]
    s1 = _Softmax.start(_dot_nt(q1, km), vm)
    s2 = _Softmax.start(_dot_nt(q2, km), vm)

    def tile(j, carry, mask):
        c1, c2 = _Softmax(*carry[0]), _Softmax(*carry[1])
        start = pl.multiple_of(j * tq, tq)
        kj = k_ref[0, pl.ds(start, tq), :]
        vj = v_ref[0, pl.ds(start, tq), :]
        a1 = _dot_nt(q1, kj)
        a2 = _dot_nt(q2, kj)
        if mask is not None:
            a1 = jnp.where(mask, a1, NEG_INF)
            a2 = jnp.where(mask, a2, NEG_INF)
        return (c1.add(a1, vj).tup(), c2.add(a2, vj).tup())

    carry = lax.fori_loop(0, i, lambda j, c: tile(j, c, None), (s1.tup(), s2.tup()))
    shift = CHUNK.bit_length() - 1
    qc = lax.broadcasted_iota(jnp.int32, (tq, tq), 0) >> shift
    kc = lax.broadcasted_iota(jnp.int32, (tq, tq), 1) >> shift
    carry = tile(i, carry, kc <= qc)
    _attn_finish(_Softmax(*carry[0]), _Softmax(*carry[1]), lp_ref, sub_ref, o_ref)


def _attn_prompt(q, k, v, km, vm, lp, sub, tq):
    b, t, d = q.shape
    qspec = pl.BlockSpec((1, tq, LANES), lambda bi, h, i: (bi, i, h))
    kspec = pl.BlockSpec((1, t, LANES), lambda bi, h, i: (bi, 0, h))
    mspec = pl.BlockSpec((N_META, LANES), lambda bi, h, i: (0, h))
    return pl.pallas_call(
        functools.partial(_attn_prompt_kernel, tq=tq),
        grid=(b, N_HEADS, t // tq),
        in_specs=[qspec, kspec, kspec, mspec, mspec,
                  pl.BlockSpec((4, HEAD_DIM), lambda bi, h, i: (0, 0)),
                  pl.BlockSpec((1, V_DIM), lambda bi, h, i: (0, 0))],
        out_specs=qspec,
        out_shape=jax.ShapeDtypeStruct((b, t, d), bf16),
        compiler_params=_cparams(("parallel", "parallel", "arbitrary")),
        name="attn_prompt",
    )(q, k, v, km, vm, lp, sub)


def _attn_full_kernel(*refs, n_cache):
    if n_cache:
        q_ref, kc_ref, vc_ref, kn_ref, vn_ref, lp_ref, sub_ref, o_ref = refs
    else:
        q_ref, kn_ref, vn_ref, lp_ref, sub_ref, o_ref = refs
    q1, q2 = _split_heads(q_ref[0])
    kn = kn_ref[0]
    vn = vn_ref[0]
    s1 = _Softmax.start(_dot_nt(q1, kn), vn)
    s2 = _Softmax.start(_dot_nt(q2, kn), vn)
    if n_cache:
        for lo, hi in ((0, N_META), (N_META, n_cache)):
            kc = kc_ref[0, lo:hi, :].astype(bf16)
            vc = vc_ref[0, lo:hi, :].astype(bf16)
            s1 = s1.add(_dot_nt(q1, kc), vc)
            s2 = s2.add(_dot_nt(q2, kc), vc)
    _attn_finish(s1, s2, lp_ref, sub_ref, o_ref)


def _attn_full(q, kn, vn, lp, sub, cache=None):
    b, tq, d = q.shape
    spec = pl.BlockSpec((1, tq, LANES), lambda bi, h: (bi, 0, h))
    ins, specs, n_cache = [q], [spec], 0
    if cache is not None:
        n_cache = cache[0].shape[1]
        cspec = pl.BlockSpec((1, n_cache, LANES), lambda bi, h: (bi, 0, h))
        ins += list(cache)
        specs += [cspec, cspec]
    ins += [kn, vn, lp, sub]
    specs += [spec, spec, pl.BlockSpec((4, HEAD_DIM), lambda bi, h: (0, 0)),
              pl.BlockSpec((1, V_DIM), lambda bi, h: (0, 0))]
    return pl.pallas_call(
        functools.partial(_attn_full_kernel, n_cache=n_cache),
        grid=(b, N_HEADS),
        in_specs=specs,
        out_specs=spec,
        out_shape=jax.ShapeDtypeStruct((b, tq, d), bf16),
        compiler_params=_cparams(("parallel", "parallel")),
        name="attn_full",
    )(*ins)


def _proj_add_kernel(x_ref, o_ref_in, w_ref, y_ref):
    y_ref[...] = x_ref[...] + _dot(o_ref_in[...], w_ref[...])


def _proj_add(x, o, w, tm):
    n, d = x.shape
    row = _rows(tm, d)
    return pl.pallas_call(
        _proj_add_kernel,
        grid=(n // tm,),
        in_specs=[row, row, _whole((d, d))],
        out_specs=row,
        out_shape=jax.ShapeDtypeStruct((n, d), f32),
        compiler_params=_cparams(("parallel",)),
        name="attn_out_proj",
    )(x, o, w)


FRAME_TILE = 512
RWKV_TILE = 256
ATTN_Q_TILE = 512
WKV_CHUNK = 64


def _rope_tables(pos):
    half = ROPE_DIM // 2
    inv = jnp.power(jnp.float32(ROPE_THETA), -jnp.arange(0, ROPE_DIM, 2, dtype=f32) / ROPE_DIM)
    ang = pos.astype(f32)[:, None] * inv[None, :]
    cos, sin = jnp.cos(ang), jnp.sin(ang)
    n = pos.shape[0]
    one = jnp.ones((n, HEAD_DIM - ROPE_DIM), f32)
    zero = jnp.zeros((n, HEAD_DIM - ROPE_DIM), f32)
    zh = jnp.zeros((n, half), f32)
    c = jnp.concatenate([cos, cos, one], axis=1)
    sa = jnp.concatenate([-sin, zh, zero], axis=1)
    sb = jnp.concatenate([zh, sin, zero], axis=1)
    return tuple(jnp.concatenate([t, t], axis=1) for t in (c, sa, sb))


def kernel(x_prompt, x_sample, cache_k, cache_v, state_wkv, state_shift, meta_tokens, norm_ffn, norm_mix, ffn_w_gate, ffn_w_up, ffn_w_down, rw_mu, rw_w_r, rw_w_k, rw_w_v, rw_w_o, rw_w0, rw_w1, rw_w2, rw_a0, rw_a1, rw_a2, rw_g1, rw_g2, rw_k_k, rw_k_a, rw_r_k, rw_gn_g, rw_gn_b, kv_norm, w_k, w_v, b_w_q, b_w_o, b_lambda, b_subln, final_norm):
    bp, tp, d = x_prompt.shape
    bd, ts, _ = x_sample.shape
    past = cache_k.shape[1] - N_META
    n_small = N_META + bd * ts
    vec = lambda a: a.reshape(1, -1).astype(f32)
    wb = lambda a: a.astype(bf16)

    head_of_lane = jnp.arange(d) // RW_HEAD
    hs1 = (head_of_lane[:, None] == jnp.arange(LANES)[None, :]).astype(bf16)
    hs2 = hs1.T
    rw = dict(norm_mix=vec(norm_mix[0]), mu=rw_mu[0], w_r=wb(rw_w_r[0]), w_k=wb(rw_w_k[0]), w_v=wb(rw_w_v[0]),
              w_o=wb(rw_w_o[0]), w0=vec(rw_w0[0]), w1=wb(rw_w1[0]), w2=wb(rw_w2[0]), a0=vec(rw_a0[0]),
              a1=wb(rw_a1[0]), a2=wb(rw_a2[0]), g1=wb(rw_g1[0]), g2=wb(rw_g2[0]), k_k=vec(rw_k_k[0]),
              k_a=vec(rw_k_a[0]), r_k=vec(rw_r_k[0]), gn_g=vec(rw_gn_g[0]), gn_b=vec(rw_gn_b[0]),
              hs1=hs1, hs2=hs2)
    ffn_w = [[(vec(norm_ffn[li, j]), wb(ffn_w_gate[li, j]), wb(ffn_w_up[li, j]), wb(ffn_w_down[li, j]))
              for j in range(2)] for li in range(2)]
    wk_b, wv_b, wq_b, wo_b = wb(w_k), wb(w_v), wb(b_w_q[0]), wb(b_w_o[0])
    lp = b_lambda[0].astype(f32)
    sub = vec(b_subln[0])
    fin = vec(final_norm)

    seq = N_META
    xs = jnp.concatenate([meta_tokens.astype(f32), x_sample.reshape(bd * ts, d)], axis=0)
    n_seq_s = n_small // seq
    shift_s0 = jnp.concatenate([jnp.zeros((1, d), f32), state_shift[:, 0]], axis=0)[:, None, :]
    wkv_s0 = jnp.concatenate([jnp.zeros((1,) + state_wkv.shape[2:], f32), state_wkv[:, 0]], axis=0)
    pos_s = jnp.concatenate([jnp.arange(N_META), jnp.tile(N_META + past + jnp.arange(ts), bd)])
    tabs_s = _rope_tables(pos_s)

    x1s = _ffn(xs, *ffn_w[0][0], tm=n_small)
    r, lw, k, v, a_s, b_s, g, hl_s = _rwkv_proj(x1s, shift_s0, rw, tm=seq, tiles_per_seq=1)
    sq = lambda z: z.reshape(n_seq_s, seq, d)
    y, st_s = _wkv(sq(r), sq(lw), sq(k), sq(v), sq(a_s), sq(b_s), _pair_states(wkv_s0), L=seq)
    x2s = _rwkv_out(x1s, y.reshape(n_small, d), r, k, v, g, rw, tm=n_small)
    x3s = _ffn(x2s, *ffn_w[0][1], tm=n_small)
    k_s, v_s, kb_s, vb_s = _kv_proj(x3s, vec(kv_norm), wk_b, wv_b, tabs_s, tm=n_small, tab_blocks=1)
    x4s = _ffn(x3s, *ffn_w[1][0], tm=n_small)
    q_s = _q_proj(x4s, vec(norm_mix[1]), wq_b, tabs_s, tm=n_small, tab_blocks=1)
    o_meta = _attn_full(q_s[None, :N_META], kb_s[None, :N_META], vb_s[None, :N_META], lp, sub)
    bs3 = lambda z: z[N_META:].reshape(bd, ts, d)
    o_samp = _attn_full(bs3(q_s), bs3(kb_s), bs3(vb_s), lp, sub,
                        cache=(cache_k.reshape(bd, N_META + past, d), cache_v.reshape(bd, N_META + past, d)))
    o_s = jnp.concatenate([o_meta[0], o_samp.reshape(bd * ts, d)], axis=0)
    x5s = _proj_add(x4s, o_s, wo_b, tm=n_small)
    ys = _ffn(x5s, *ffn_w[1][1], tm=n_small, final_g=fin)

    n_f = bp * tp
    xf = x_prompt.reshape(n_f, d)
    tabs_f = _rope_tables(N_META + jnp.arange(tp))
    shift_f0 = jnp.broadcast_to(hl_s[0:1], (bp, 1, d))
    wkv_f0 = jnp.broadcast_to(st_s[0:1], (bp,) + st_s.shape[1:])

    x1 = _ffn(xf, *ffn_w[0][0], tm=FRAME_TILE)
    r, lw, k, v, a_s, b_s, g, hl_f = _rwkv_proj(x1, shift_f0, rw, tm=RWKV_TILE, tiles_per_seq=tp // RWKV_TILE)
    sq = lambda z: z.reshape(bp, tp, d)
    y, st_f = _wkv(sq(r), sq(lw), sq(k), sq(v), sq(a_s), sq(b_s), wkv_f0, L=WKV_CHUNK)
    x2 = _rwkv_out(x1, y.reshape(n_f, d), r, k, v, g, rw, tm=RWKV_TILE)
    x3 = _ffn(x2, *ffn_w[0][1], tm=FRAME_TILE)
    k_f, v_f, kb_f, vb_f = _kv_proj(x3, vec(kv_norm), wk_b, wv_b, tabs_f, tm=FRAME_TILE, tab_blocks=tp // FRAME_TILE)
    x4 = _ffn(x3, *ffn_w[1][0], tm=FRAME_TILE)
    q_f = _q_proj(x4, vec(norm_mix[1]), wq_b, tabs_f, tm=FRAME_TILE, tab_blocks=tp // FRAME_TILE)
    o_f = _attn_prompt(sq(q_f), sq(kb_f), sq(vb_f), kb_s[:N_META], vb_s[:N_META], lp, sub, tq=ATTN_Q_TILE)
    x5 = _proj_add(x4, o_f.reshape(n_f, d), wo_b, tm=FRAME_TILE)
    yf = _ffn(x5, *ffn_w[1][1], tm=FRAME_TILE, final_g=fin)

    y_prompt = yf.reshape(bp, tp, d)
    y_sample = ys[N_META:].reshape(bd, ts, d)
    wkv_p = _head_states(st_f)[:, None]
    wkv_s = _head_states(st_s[1:])[:, None]
    last_tile = tp // RWKV_TILE - 1
    shift_p = hl_f.reshape(bp, tp // RWKV_TILE, d)[:, last_tile][:, None]
    shift_s = hl_s[1:, 0][:, None]
    meta_rows = lambda z: jnp.broadcast_to(z[None, :N_META], (bp, N_META, d))
    new_k_p = jnp.concatenate([meta_rows(k_s), k_f.reshape(bp, tp, d)], axis=1).reshape(bp, N_META + tp, N_HEADS, V_DIM)
    new_v_p = jnp.concatenate([meta_rows(v_s), v_f.reshape(bp, tp, d)], axis=1).reshape(bp, N_META + tp, N_HEADS, V_DIM)
    new_k_s = k_s[N_META:].reshape(bd, ts, N_HEADS, V_DIM)
    new_v_s = v_s[N_META:].reshape(bd, ts, N_HEADS, V_DIM)
    return (y_prompt, y_sample, wkv_p, shift_p, new_k_p, new_v_p, wkv_s, shift_s, new_k_s, new_v_s)
```

```python
import functools
import math

import jax
import jax.numpy as jnp
from jax import lax
from jax.experimental import pallas as pl
from jax.experimental.pallas import tpu as pltpu

f32 = jnp.float32
bf16 = jnp.bfloat16

D_MODEL = 1024
CHUNK = 64
N_META = 16
RW_HEAD = 64
RW_HEADS = D_MODEL // RW_HEAD
GN_EPS = 64e-5
HEAD_DIM = 64
N_HEADS = D_MODEL // (2 * HEAD_DIM)
V_DIM = 2 * HEAD_DIM
ROPE_DIM = HEAD_DIM // 4
ROPE_THETA = 500000.0
SUBLN_EPS = 1e-5
NEG_INF = -1e30
NORM_EPS = 1e-6
LAM_INIT = 0.8 - 0.6 * math.exp(-0.3 * 1)
LOG2E = math.log2(math.e)
CHUNK_SHIFT = CHUNK.bit_length() - 1
SOFTMAX_ROWS = 32

LANES = 128
HEADS_PER_VREG = LANES // RW_HEAD
N_PAIRS = D_MODEL // LANES
VMEM_LIMIT = 56 * 1024 * 1024


def _cparams(sem):
    return pltpu.CompilerParams(dimension_semantics=sem, vmem_limit_bytes=VMEM_LIMIT)


def _dot(a, b):
    return jnp.dot(a, b, preferred_element_type=f32)


def _dot_nt(a, b):
    return lax.dot_general(a, b, (((1,), (1,)), ((), ())), preferred_element_type=f32)


def _dot_tn(a, b):
    return lax.dot_general(a, b, (((0,), (0,)), ((), ())), preferred_element_type=f32)


def _split2(x):
    hi = x.astype(bf16)
    lo = (x - hi.astype(f32)).astype(bf16)
    return hi, lo


def _split3(x):
    h1 = x.astype(bf16)
    r1 = x - h1.astype(f32)
    h2 = r1.astype(bf16)
    h3 = (r1 - h2.astype(f32)).astype(bf16)
    return h1, h2, h3


def _sigmoid(x):
    return 1.0 / (1.0 + jnp.exp(-x))


def _rms(x, g, eps):
    return x * lax.rsqrt(jnp.mean(x * x, axis=-1, keepdims=True) + eps) * g


def _headsum(x, g1, g2):
    hi, lo = _split2(x)
    s = _dot(hi, g1) + _dot(lo, g1)
    shi, slo = _split2(s)
    return _dot(shi, g2) + _dot(slo, g2)


def _rope(x, cos, sin_a, sin_b):
    reps = x.shape[1] // LANES
    c = jnp.concatenate([cos] * reps, axis=1)
    sa = jnp.concatenate([sin_a] * reps, axis=1)
    sb = jnp.concatenate([sin_b] * reps, axis=1)
    half = ROPE_DIM // 2
    return x * c + pltpu.roll(x, x.shape[1] - half, 1) * sa + pltpu.roll(x, half, 1) * sb


def _row(i):
    return (i, 0)


def _fixed2(i):
    return (0, 0)


def _rows(tm, w):
    return pl.BlockSpec((tm, w), _row)


def _whole(shape):
    return pl.BlockSpec(shape, _fixed2, pipeline_mode=pl.Buffered(1))


def _ffn_kernel(x_ref, g_ref, wg_ref, wu_ref, wd_ref, *rest, final_norm):
    o_ref = rest[-1]
    x = x_ref[...]
    xb = _rms(x, g_ref[...], NORM_EPS).astype(bf16)
    gate = _dot(xb, wg_ref[...])
    up = _dot(xb, wu_ref[...])
    act = (gate * _sigmoid(gate) * up).astype(bf16)
    y = x + 0.5 * _dot(act, wd_ref[...])
    if final_norm:
        y = _rms(y, rest[0][...], NORM_EPS)
    o_ref[...] = y


def _ffn(x, g, wg, wu, wd, tm, final_g=None):
    n, d = x.shape
    dff = wg.shape[1]
    ins = [x, g, wg, wu, wd]
    specs = [_rows(tm, d), _whole((1, d)), _whole((d, dff)), _whole((d, dff)), _whole((dff, d))]
    if final_g is not None:
        ins.append(final_g)
        specs.append(_whole((1, d)))
    return pl.pallas_call(
        functools.partial(_ffn_kernel, final_norm=final_g is not None),
        grid=(n // tm,),
        in_specs=specs,
        out_specs=_rows(tm, d),
        out_shape=jax.ShapeDtypeStruct((n, d), f32),
        compiler_params=_cparams(("parallel",)),
        name="ffn_half",
    )(*ins)


def _rwkv_proj_kernel(x_ref, xprev_ref, shift_ref, nm_ref, mu_ref, wr_ref, wk_ref, wv_ref,
                      w0_ref, w1_ref, w2_ref, a0_ref, a1_ref, a2_ref, g1_ref, g2_ref,
                      kk_ref, ka_ref, hs1_ref, hs2_ref,
                      r_ref, lw_ref, k_ref, v_ref, as_ref, bs_ref, g_ref, hlast_ref,
                      h_scr, *, tm, tiles_per_seq):
    nm = nm_ref[...]
    h = _rms(x_ref[...], nm, NORM_EPS)
    hlast_ref[0] = h[tm - 1:tm, :]
    first = (pl.program_id(0) % tiles_per_seq) == 0
    h_before = _rms(xprev_ref[7:8, :], nm, NORM_EPS)
    h_scr[8:tm + 8, :] = h
    h_scr[7:8, :] = jnp.where(first, shift_ref[0], h_before)
    xx = h_scr[7:tm + 7, :] - h
    mu = mu_ref[...]

    def mix(m):
        return (h + xx * mu[m:m + 1, :]).astype(bf16)

    r = _dot(mix(0), wr_ref[...])
    k = _dot(mix(2), wk_ref[...])
    v = _dot(mix(3), wv_ref[...])
    ww = w0_ref[...] + _dot(jnp.tanh(_dot(mix(1), w1_ref[...])).astype(bf16), w2_ref[...])
    neg = -ww
    softplus = jnp.maximum(neg, 0.0) + jnp.log(1.0 + jnp.exp(-jnp.abs(neg)))
    lw = -jnp.exp(-softplus - 0.5)
    a = _sigmoid(a0_ref[...] + _dot(_dot(mix(4), a1_ref[...]).astype(bf16), a2_ref[...]))
    g = _dot(_sigmoid(_dot(mix(5), g1_ref[...])).astype(bf16), g2_ref[...])
    kk = k * kk_ref[...]
    norm = jnp.sqrt(_headsum(kk * kk, hs1_ref[...], hs2_ref[...]))
    kk = kk / jnp.maximum(norm, 1e-12)
    r_ref[...] = r
    lw_ref[...] = lw
    k_ref[...] = k * (1.0 + (a - 1.0) * ka_ref[...])
    v_ref[...] = v
    as_ref[...] = -kk
    bs_ref[...] = kk * a
    g_ref[...] = g


def _rwkv_proj(x, shift0, p, tm, tiles_per_seq):
    n, d = x.shape
    nt = n // tm
    row = _rows(tm, d)
    lw_, la_, lg_ = p["w1"].shape[1], p["a1"].shape[1], p["g1"].shape[1]
    specs = [
        row,
        pl.BlockSpec((8, d), lambda i: (jnp.maximum(i * (tm // 8) - 1, 0), 0)),
        pl.BlockSpec((1, 1, d), lambda i: (i // tiles_per_seq, 0, 0)),
        _whole((1, d)), _whole((6, d)),
        _whole((d, d)), _whole((d, d)), _whole((d, d)),
        _whole((1, d)), _whole((d, lw_)), _whole((lw_, d)),
        _whole((1, d)), _whole((d, la_)), _whole((la_, d)),
        _whole((d, lg_)), _whole((lg_, d)),
        _whole((1, d)), _whole((1, d)), _whole((d, LANES)), _whole((LANES, d)),
    ]
    outs = [jax.ShapeDtypeStruct((n, d), f32)] * 7 + [jax.ShapeDtypeStruct((nt, 1, d), f32)]
    out_specs = [row] * 7 + [pl.BlockSpec((1, 1, d), lambda i: (i, 0, 0))]
    return pl.pallas_call(
        functools.partial(_rwkv_proj_kernel, tm=tm, tiles_per_seq=tiles_per_seq),
        grid=(nt,),
        in_specs=specs,
        out_specs=out_specs,
        out_shape=outs,
        scratch_shapes=[pltpu.VMEM((tm + 8, d), f32)],
        compiler_params=_cparams(("parallel",)),
        name="rwkv_proj",
    )(x, x, shift0, p["norm_mix"], p["mu"], p["w_r"], p["w_k"], p["w_v"],
      p["w0"], p["w1"], p["w2"], p["a0"], p["a1"], p["a2"], p["g1"], p["g2"],
      p["k_k"], p["k_a"], p["hs1"], p["hs2"])


def _wkv_kernel(r_ref, lw_ref, k_ref, v_ref, a_ref, b_ref, s0_ref, y_ref, sT_ref, s_scr, *, L):
    c = pl.program_id(1)

    @pl.when(c == 0)
    def _():
        s_scr[...] = s0_ref[0]

    L2 = 2 * L
    ri = lax.broadcasted_iota(jnp.int32, (L, L), 0)
    ci = lax.broadcasted_iota(jnp.int32, (L, L), 1)
    cum = jnp.where(ri >= ci, 1.0, 0.0).astype(bf16)
    rs = lax.broadcasted_iota(jnp.int32, (L2, LANES), 0)
    ls = lax.broadcasted_iota(jnp.int32, (L2, LANES), 1)
    own = (rs < L) == (ls < RW_HEAD)
    rg = lax.broadcasted_iota(jnp.int32, (L2, L2), 0) & (L - 1)
    cg = lax.broadcasted_iota(jnp.int32, (L2, L2), 1) & (L - 1)
    strict = rg > cg
    incl = rg >= cg

    lw = lw_ref[0]
    l1, l2, l3 = _split3(lw)
    cs = _dot(cum, l1) + _dot(cum, l2) + _dot(cum, l3)
    w_in = jnp.exp(cs)
    w_inv = jnp.exp(-cs)
    pairs = range(N_PAIRS)

    def stack(x):
        out = []
        for p in pairs:
            xp = x[:, p * LANES:(p + 1) * LANES]
            out.append(jnp.where(own, jnp.concatenate([xp, xp], axis=0), 0.0).astype(bf16))
        return out

    a_st = stack(a_ref[0] * jnp.exp(cs - lw))
    r_st = stack(r_ref[0] * w_in)
    b_st = stack(b_ref[0] * w_inv)
    k_st = stack(k_ref[0] * w_inv)
    v_st = stack(v_ref[0])
    gram = [_dot_nt(jnp.concatenate([a_st[p], r_st[p]], axis=0), jnp.concatenate([b_st[p], k_st[p]], axis=0))
            for p in pairs]
    m = [jnp.where(strict, gram[p][:L2, :L2], 0.0) for p in pairs]
    a_ak = [jnp.where(strict, gram[p][:L2, L2:], 0.0).astype(bf16) for p in pairs]
    a_rb = [jnp.where(incl, gram[p][L2:, :L2], 0.0).astype(bf16) for p in pairs]
    a_rk = [jnp.where(incl, gram[p][L2:, L2:], 0.0).astype(bf16) for p in pairs]
    s = [s_scr[p] for p in pairs]
    sb = [s[p].astype(bf16) for p in pairs]
    rhs = [_dot_nt(a_st[p], sb[p]) + _dot(a_ak[p], v_st[p]) for p in pairs]
    t = m
    for _ in range(L.bit_length() - 2):
        mb = [m[p].astype(bf16) for p in pairs]
        m = [_dot(mb[p], mb[p]) for p in pairs]
        t = [t[p] + m[p] + _dot(t[p].astype(bf16), m[p].astype(bf16)) for p in pairs]
    ub = [(rhs[p] + _dot(t[p].astype(bf16), rhs[p].astype(bf16))).astype(bf16) for p in pairs]
    y_st = [_dot_nt(r_st[p], sb[p]) + _dot(a_rb[p], ub[p]) + _dot(a_rk[p], v_st[p]) for p in pairs]
    y_ref[0] = jnp.concatenate([y_st[p][:L] + y_st[p][L:] for p in pairs], axis=1)
    for p in pairs:
        w_last = w_in[L - 1:L, p * LANES:(p + 1) * LANES]
        s_scr[p] = (s[p] + _dot_tn(ub[p], b_st[p]) + _dot_tn(v_st[p], k_st[p])) * w_last

    @pl.when(c == pl.num_programs(1) - 1)
    def _():
        sT_ref[0] = s_scr[...]


def _wkv(r, lw, k, v, a, b, s0, L):
    nseq, t, d = r.shape
    blk = pl.BlockSpec((1, L, d), lambda s, c: (s, c, 0))
    st = pl.BlockSpec((1, N_PAIRS, LANES, LANES), lambda s, c: (s, 0, 0, 0))
    return pl.pallas_call(
        functools.partial(_wkv_kernel, L=L),
        grid=(nseq, t // L),
        in_specs=[blk] * 6 + [st],
        out_specs=[blk, st],
        out_shape=[jax.ShapeDtypeStruct((nseq, t, d), f32),
                   jax.ShapeDtypeStruct((nseq, N_PAIRS, LANES, LANES), f32)],
        scratch_shapes=[pltpu.VMEM((N_PAIRS, LANES, LANES), f32)],
        compiler_params=_cparams(("parallel", "arbitrary")),
        name="wkv_scan",
    )(r, lw, k, v, a, b, s0)


def _pair_states(s):
    n = s.shape[0]
    s = s.reshape(n, N_PAIRS, HEADS_PER_VREG, RW_HEAD, RW_HEAD)
    eye = jnp.eye(HEADS_PER_VREG, dtype=s.dtype)
    return jnp.einsum("npivk,ij->npivjk", s, eye).reshape(n, N_PAIRS, LANES, LANES)


def _head_states(s):
    n = s.shape[0]
    s = s.reshape(n, N_PAIRS, HEADS_PER_VREG, RW_HEAD, HEADS_PER_VREG, RW_HEAD)
    s = jnp.stack([s[:, :, i, :, i, :] for i in range(HEADS_PER_VREG)], axis=2)
    return s.reshape(n, RW_HEADS, RW_HEAD, RW_HEAD)


def _rwkv_out_kernel(x_ref, y_ref, r_ref, k_ref, v_ref, g_ref, rk_ref, gg_ref, gb_ref,
                     wo_ref, hs1_ref, hs2_ref, o_ref):
    hs1 = hs1_ref[...]
    hs2 = hs2_ref[...]
    y = y_ref[...]
    d = y - _headsum(y, hs1, hs2) * (1.0 / RW_HEAD)
    var = _headsum(d * d, hs1, hs2) * (1.0 / RW_HEAD)
    yn = d * lax.rsqrt(var + GN_EPS) * gg_ref[...] + gb_ref[...]
    v = v_ref[...]
    bonus = _headsum(r_ref[...] * k_ref[...] * rk_ref[...], hs1, hs2) * v
    o_ref[...] = x_ref[...] + _dot(((yn + bonus) * g_ref[...]).astype(bf16), wo_ref[...])


def _rwkv_out(x, y, r, k, v, g, p, tm):
    n, d = x.shape
    row = _rows(tm, d)
    return pl.pallas_call(
        _rwkv_out_kernel,
        grid=(n // tm,),
        in_specs=[row] * 6 + [_whole((1, d))] * 3 + [_whole((d, d)), _whole((d, LANES)), _whole((LANES, d))],
        out_specs=row,
        out_shape=jax.ShapeDtypeStruct((n, d), f32),
        compiler_params=_cparams(("parallel",)),
        name="rwkv_out",
    )(x, y, r, k, v, g, p["r_k"], p["gn_g"], p["gn_b"], p["w_o"], p["hs1"], p["hs2"])


def _kv_proj_kernel(x_ref, g_ref, wk_ref, wv_ref, cos_ref, sa_ref, sb_ref, k_ref, v_ref, kb_ref, vb_ref,
                    *, cache_layout):
    hb = _rms(x_ref[...], g_ref[...], NORM_EPS).astype(bf16)
    k = _rope(_dot(hb, wk_ref[...]), cos_ref[...], sa_ref[...], sb_ref[...])
    v = _dot(hb, wv_ref[...])
    kb_ref[...] = k.astype(bf16)
    vb_ref[...] = v.astype(bf16)
    if cache_layout:
        tm = k.shape[0]
        for h in range(N_HEADS):
            k_ref[0, pl.ds(h, tm, stride=N_HEADS), :] = k[:, h * V_DIM:(h + 1) * V_DIM]
            v_ref[0, pl.ds(h, tm, stride=N_HEADS), :] = v[:, h * V_DIM:(h + 1) * V_DIM]
    else:
        k_ref[...] = k
        v_ref[...] = v


def _kv_proj(x, g, wk, wv, rope_tabs, tm, tab_blocks, cache_rows=None):
    n, d = x.shape
    row = _rows(tm, d)
    tab = pl.BlockSpec((tm, LANES), lambda i: (i % tab_blocks, 0))
    if cache_rows is None:
        f32_spec = row
        f32_shape = jax.ShapeDtypeStruct((n, d), f32)
    else:
        n_seq, rows_per_seq, first_row = cache_rows
        tiles_per_seq = n // (n_seq * tm)
        f32_spec = pl.BlockSpec(
            (pl.Element(1), pl.Element(tm * N_HEADS), pl.Element(V_DIM)),
            lambda i: (i // tiles_per_seq, (first_row + (i % tiles_per_seq) * tm) * N_HEADS, 0))
        f32_shape = jax.ShapeDtypeStruct((n_seq, rows_per_seq * N_HEADS, V_DIM), f32)
    return pl.pallas_call(
        functools.partial(_kv_proj_kernel, cache_layout=cache_rows is not None),
        grid=(n // tm,),
        in_specs=[row, _whole((1, d)), _whole((d, d)), _whole((d, d)), tab, tab, tab],
        out_specs=[f32_spec, f32_spec, row, row],
        out_shape=[f32_shape] * 2 + [jax.ShapeDtypeStruct((n, d), bf16)] * 2,
        compiler_params=_cparams(("parallel",)),
        name="kv_proj",
    )(x, g, wk, wv, *rope_tabs)


def _put_rows_kernel(big_ref, rows_ref, o_ref):
    del big_ref
    o_ref[0] = rows_ref[...]


def _put_rows(big, rows):
    n_seq = big.shape[0]
    r = rows.shape[0]
    return pl.pallas_call(
        _put_rows_kernel,
        grid=(n_seq,),
        in_specs=[pl.BlockSpec(memory_space=pl.ANY), pl.BlockSpec((r, LANES), lambda s: (0, 0))],
        out_specs=pl.BlockSpec((1, r, LANES), lambda s: (s, 0, 0)),
        out_shape=jax.ShapeDtypeStruct(big.shape, big.dtype),
        input_output_aliases={0: 0},
        compiler_params=_cparams(("parallel",)),
        name="put_meta_rows",
    )(big, rows)


def _q_proj_kernel(x_ref, g_ref, wq_ref, cos_ref, sa_ref, sb_ref, q_ref):
    hb = _rms(x_ref[...], g_ref[...], NORM_EPS).astype(bf16)
    q = _rope(_dot(hb, wq_ref[...]), cos_ref[...], sa_ref[...], sb_ref[...])
    q_ref[...] = (q * (HEAD_DIM ** -0.5 * LOG2E)).astype(bf16)


def _q_proj(x, g, wq, rope_tabs, tm, tab_blocks):
    n, d = x.shape
    row = _rows(tm, d)
    tab = pl.BlockSpec((tm, LANES), lambda i: (i % tab_blocks, 0))
    return pl.pallas_call(
        _q_proj_kernel,
        grid=(n // tm,),
        in_specs=[row, _whole((1, d)), _whole((d, d)), tab, tab, tab],
        out_specs=row,
        out_shape=jax.ShapeDtypeStruct((n, d), bf16),
        compiler_params=_cparams(("parallel",)),
        name="q_proj",
    )(x, g, wq, *rope_tabs)


def _lam(lp):
    e1 = jnp.exp(jnp.sum(lp[0:1, :] * lp[1:2, :], axis=-1, keepdims=True))
    e2 = jnp.exp(jnp.sum(lp[2:3, :] * lp[3:4, :], axis=-1, keepdims=True))
    return e1 - e2 + LAM_INIT


def _split_heads(q):
    lane = lax.broadcasted_iota(jnp.int32, q.shape, 1)
    zero = jnp.zeros_like(q)
    return jnp.where(lane < HEAD_DIM, q, zero), jnp.where(lane >= HEAD_DIM, q, zero)


def _stack_components(q):
    q1, q2 = _split_heads(q)
    return jnp.concatenate([q1, q2], axis=0)


def _with_ones(v):
    return jnp.concatenate([v, jnp.ones(v.shape, v.dtype)], axis=1)


def _attn_finish(acc, lp, sub):
    n = acc.shape[0] // 2
    o = acc[:n, :V_DIM] / acc[:n, V_DIM:] - _lam(lp) * (acc[n:, :V_DIM] / acc[n:, V_DIM:])
    o = o * lax.rsqrt(jnp.mean(o * o, axis=-1, keepdims=True) + SUBLN_EPS) * sub
    return o * (1.0 - LAM_INIT)


def _attn_prompt_kernel(q_ref, k_ref, v_ref, km_ref, vm_ref, lp_ref, sub_ref, o_ref, *, tq):
    i = pl.program_id(2)
    qs = _split_heads(q_ref[0])
    km = km_ref[...]
    vm = vm_ref[...]
    carry = []
    for q in qs:
        sm = _dot_nt(q, km)
        m0 = jnp.max(sm, axis=-1, keepdims=True)
        p0 = jnp.exp2(sm - m0)
        carry += [m0, jnp.sum(p0, axis=-1, keepdims=True), _dot(p0.astype(bf16), vm)]

    def tile(j, carry, mask):
        start = pl.multiple_of(j * tq, tq)
        kj = k_ref[0, pl.ds(start, tq), :]
        vj = v_ref[0, pl.ds(start, tq), :]
        out = []
        for c, q in enumerate(qs):
            m_old, l_old, acc = carry[3 * c:3 * c + 3]
            s = _dot_nt(q, kj)
            if mask is not None:
                s = jnp.where(mask, s, NEG_INF)
            m_new = jnp.maximum(m_old, jnp.max(s, axis=-1, keepdims=True))
            alpha = jnp.exp2(m_old - m_new)
            p = jnp.exp2(s - m_new)
            out += [m_new, alpha * l_old + jnp.sum(p, axis=-1, keepdims=True),
                    alpha * acc + _dot(p.astype(bf16), vj)]
        return tuple(out)

    carry = lax.fori_loop(0, i, lambda j, c: tile(j, c, None), tuple(carry))
    q_chunk = lax.broadcasted_iota(jnp.int32, (tq, tq), 0) >> CHUNK_SHIFT
    k_chunk = lax.broadcasted_iota(jnp.int32, (tq, tq), 1) >> CHUNK_SHIFT
    carry = tile(i, carry, k_chunk <= q_chunk)
    acc = jnp.concatenate([jnp.concatenate([carry[3 * c + 2], jnp.broadcast_to(carry[3 * c + 1], (tq, V_DIM))], axis=1)
                           for c in range(2)], axis=0)
    o_ref[0] = _attn_finish(acc, lp_ref[...], sub_ref[...]).astype(o_ref.dtype)


def _attn_prompt(q, k, v, km, vm, lp, sub, tq):
    b, t, d = q.shape
    qspec = pl.BlockSpec((1, tq, LANES), lambda bi, h, i: (bi, i, h))
    kspec = pl.BlockSpec((1, t, LANES), lambda bi, h, i: (bi, 0, h))
    mspec = pl.BlockSpec((N_META, LANES), lambda bi, h, i: (0, h))
    return pl.pallas_call(
        functools.partial(_attn_prompt_kernel, tq=tq),
        grid=(b, N_HEADS, t // tq),
        in_specs=[qspec, kspec, kspec, mspec, mspec,
                  pl.BlockSpec((4, HEAD_DIM), lambda bi, h, i: (0, 0)),
                  pl.BlockSpec((1, V_DIM), lambda bi, h, i: (0, 0))],
        out_specs=qspec,
        out_shape=jax.ShapeDtypeStruct((b, t, d), bf16),
        compiler_params=_cparams(("parallel", "parallel", "arbitrary")),
        name="attn_prompt",
    )(q, k, v, km, vm, lp, sub)


def _attn_full_kernel(*refs, n_cache):
    if n_cache:
        q_ref, kc_ref, vc_ref, kn_ref, vn_ref, lp_ref, sub_ref, o_ref = refs
    else:
        q_ref, kn_ref, vn_ref, lp_ref, sub_ref, o_ref = refs
    lp = lp_ref[...]
    sub = sub_ref[...]
    for h in range(N_HEADS):
        cols = slice(h * V_DIM, (h + 1) * V_DIM)
        qq = _stack_components(q_ref[0, :, cols])
        keys = [kn_ref[0, :, cols]]
        vals = [vn_ref[0, :, cols]]
        if n_cache:
            for first, count in ((0, N_META), (N_META, n_cache - N_META)):
                rows = pl.ds(first * N_HEADS + h, count, stride=N_HEADS)
                keys.append(kc_ref[0, rows, :].astype(bf16))
                vals.append(vc_ref[0, rows, :].astype(bf16))
        scores = [_dot_nt(qq, kk) for kk in keys]
        m = functools.reduce(jnp.maximum, [jnp.max(s, axis=-1, keepdims=True) for s in scores])
        acc = sum(_dot(jnp.exp2(s - m).astype(bf16), _with_ones(vv)) for s, vv in zip(scores, vals))
        o_ref[0, :, cols] = _attn_finish(acc, lp, sub).astype(o_ref.dtype)


def _attn_full(q, kn, vn, lp, sub, cache=None):
    b, tq, d = q.shape
    spec = pl.BlockSpec((1, tq, d), lambda bi: (bi, 0, 0))
    ins, specs, n_cache = [q], [spec], 0
    if cache is not None:
        n_cache = cache[0].shape[1] // N_HEADS
        cspec = pl.BlockSpec((1, n_cache * N_HEADS, V_DIM), lambda bi: (bi, 0, 0))
        ins += list(cache)
        specs += [cspec, cspec]
    ins += [kn, vn, lp, sub]
    specs += [spec, spec, pl.BlockSpec((4, HEAD_DIM), lambda bi: (0, 0)),
              pl.BlockSpec((1, V_DIM), lambda bi: (0, 0))]
    return pl.pallas_call(
        functools.partial(_attn_full_kernel, n_cache=n_cache),
        grid=(b,),
        in_specs=specs,
        out_specs=spec,
        out_shape=jax.ShapeDtypeStruct((b, tq, d), bf16),
        compiler_params=_cparams(("parallel",)),
        name="attn_full",
    )(*ins)


def _proj_add_kernel(x_ref, o_ref_in, w_ref, y_ref):
    y_ref[...] = x_ref[...] + _dot(o_ref_in[...], w_ref[...])


def _proj_add(x, o, w, tm):
    n, d = x.shape
    row = _rows(tm, d)
    return pl.pallas_call(
        _proj_add_kernel,
        grid=(n // tm,),
        in_specs=[row, row, _whole((d, d))],
        out_specs=row,
        out_shape=jax.ShapeDtypeStruct((n, d), f32),
        compiler_params=_cparams(("parallel",)),
        name="attn_out_proj",
    )(x, o, w)


FRAME_TILE = 512
RWKV_TILE = 256
ATTN_Q_TILE = 512
WKV_CHUNK = 64


def _rope_tables(pos):
    half = ROPE_DIM // 2
    inv = jnp.power(jnp.float32(ROPE_THETA), -jnp.arange(0, ROPE_DIM, 2, dtype=f32) / ROPE_DIM)
    ang = pos.astype(f32)[:, None] * inv[None, :]
    cos, sin = jnp.cos(ang), jnp.sin(ang)
    n = pos.shape[0]
    one = jnp.ones((n, HEAD_DIM - ROPE_DIM), f32)
    zero = jnp.zeros((n, HEAD_DIM - ROPE_DIM), f32)
    zh = jnp.zeros((n, half), f32)
    c = jnp.concatenate([cos, cos, one], axis=1)
    sa = jnp.concatenate([-sin, zh, zero], axis=1)
    sb = jnp.concatenate([zh, sin, zero], axis=1)
    return tuple(jnp.concatenate([t, t], axis=1) for t in (c, sa, sb))


def kernel(x_prompt, x_sample, cache_k, cache_v, state_wkv, state_shift, meta_tokens, norm_ffn, norm_mix, ffn_w_gate, ffn_w_up, ffn_w_down, rw_mu, rw_w_r, rw_w_k, rw_w_v, rw_w_o, rw_w0, rw_w1, rw_w2, rw_a0, rw_a1, rw_a2, rw_g1, rw_g2, rw_k_k, rw_k_a, rw_r_k, rw_gn_g, rw_gn_b, kv_norm, w_k, w_v, b_w_q, b_w_o, b_lambda, b_subln, final_norm):
    bp, tp, d = x_prompt.shape
    bd, ts, _ = x_sample.shape
    past = cache_k.shape[1] - N_META
    n_small = N_META + bd * ts
    vec = lambda a: a.reshape(1, -1).astype(f32)
    wb = lambda a: a.astype(bf16)

    head_of_lane = jnp.arange(d) // RW_HEAD
    hs1 = (head_of_lane[:, None] == jnp.arange(LANES)[None, :]).astype(bf16)
    hs2 = hs1.T
    rw = dict(norm_mix=vec(norm_mix[0]), mu=rw_mu[0], w_r=wb(rw_w_r[0]), w_k=wb(rw_w_k[0]), w_v=wb(rw_w_v[0]),
              w_o=wb(rw_w_o[0]), w0=vec(rw_w0[0]), w1=wb(rw_w1[0]), w2=wb(rw_w2[0]), a0=vec(rw_a0[0]),
              a1=wb(rw_a1[0]), a2=wb(rw_a2[0]), g1=wb(rw_g1[0]), g2=wb(rw_g2[0]), k_k=vec(rw_k_k[0]),
              k_a=vec(rw_k_a[0]), r_k=vec(rw_r_k[0]), gn_g=vec(rw_gn_g[0]), gn_b=vec(rw_gn_b[0]),
              hs1=hs1, hs2=hs2)
    ffn_w = [[(vec(norm_ffn[li, j]), wb(ffn_w_gate[li, j]), wb(ffn_w_up[li, j]), wb(ffn_w_down[li, j]))
              for j in range(2)] for li in range(2)]
    wk_b, wv_b, wq_b, wo_b = wb(w_k), wb(w_v), wb(b_w_q[0]), wb(b_w_o[0])
    lp = b_lambda[0].astype(f32)
    sub = vec(b_subln[0])
    fin = vec(final_norm)

    seq = N_META
    xs = jnp.concatenate([meta_tokens.astype(f32), x_sample.reshape(bd * ts, d)], axis=0)
    n_seq_s = n_small // seq
    shift_s0 = jnp.concatenate([jnp.zeros((1, d), f32), state_shift[:, 0]], axis=0)[:, None, :]
    wkv_s0 = jnp.concatenate([jnp.zeros((1,) + state_wkv.shape[2:], f32), state_wkv[:, 0]], axis=0)
    pos_s = jnp.concatenate([jnp.arange(N_META), jnp.tile(N_META + past + jnp.arange(ts), bd)])
    tabs_s = _rope_tables(pos_s)

    x1s = _ffn(xs, *ffn_w[0][0], tm=n_small)
    r, lw, k, v, a_s, b_s, g, hl_s = _rwkv_proj(x1s, shift_s0, rw, tm=seq, tiles_per_seq=1)
    sq = lambda z: z.reshape(n_seq_s, seq, d)
    y, st_s = _wkv(sq(r), sq(lw), sq(k), sq(v), sq(a_s), sq(b_s), _pair_states(wkv_s0), L=seq)
    x2s = _rwkv_out(x1s, y.reshape(n_small, d), r, k, v, g, rw, tm=n_small)
    x3s = _ffn(x2s, *ffn_w[0][1], tm=n_small)
    k_s, v_s, kb_s, vb_s = _kv_proj(x3s, vec(kv_norm), wk_b, wv_b, tabs_s, tm=n_small, tab_blocks=1)
    x4s = _ffn(x3s, *ffn_w[1][0], tm=n_small)
    q_s = _q_proj(x4s, vec(norm_mix[1]), wq_b, tabs_s, tm=n_small, tab_blocks=1)
    o_meta = _attn_full(q_s[None, :N_META], kb_s[None, :N_META], vb_s[None, :N_META], lp, sub)
    bs3 = lambda z: z[N_META:].reshape(bd, ts, d)
    cache_rows = lambda z: z.reshape(bd, (N_META + past) * N_HEADS, V_DIM)
    o_samp = _attn_full(bs3(q_s), bs3(kb_s), bs3(vb_s), lp, sub, cache=(cache_rows(cache_k), cache_rows(cache_v)))
    o_s = jnp.concatenate([o_meta[0], o_samp.reshape(bd * ts, d)], axis=0)
    x5s = _proj_add(x4s, o_s, wo_b, tm=n_small)
    ys = _ffn(x5s, *ffn_w[1][1], tm=n_small, final_g=fin)

    n_f = bp * tp
    xf = x_prompt.reshape(n_f, d)
    tabs_f = _rope_tables(N_META + jnp.arange(tp))
    shift_f0 = jnp.broadcast_to(hl_s[0:1], (bp, 1, d))
    wkv_f0 = jnp.broadcast_to(st_s[0:1], (bp,) + st_s.shape[1:])

    x1 = _ffn(xf, *ffn_w[0][0], tm=FRAME_TILE)
    r, lw, k, v, a_s, b_s, g, hl_f = _rwkv_proj(x1, shift_f0, rw, tm=RWKV_TILE, tiles_per_seq=tp // RWKV_TILE)
    sq = lambda z: z.reshape(bp, tp, d)
    y, st_f = _wkv(sq(r), sq(lw), sq(k), sq(v), sq(a_s), sq(b_s), wkv_f0, L=WKV_CHUNK)
    x2 = _rwkv_out(x1, y.reshape(n_f, d), r, k, v, g, rw, tm=RWKV_TILE)
    x3 = _ffn(x2, *ffn_w[0][1], tm=FRAME_TILE)
    k_f, v_f, kb_f, vb_f = _kv_proj(x3, vec(kv_norm), wk_b, wv_b, tabs_f, tm=FRAME_TILE, tab_blocks=tp // FRAME_TILE,
                                    cache_rows=(bp, N_META + tp, N_META))
    x4 = _ffn(x3, *ffn_w[1][0], tm=FRAME_TILE)
    q_f = _q_proj(x4, vec(norm_mix[1]), wq_b, tabs_f, tm=FRAME_TILE, tab_blocks=tp // FRAME_TILE)
    o_f = _attn_prompt(sq(q_f), sq(kb_f), sq(vb_f), kb_s[:N_META], vb_s[:N_META], lp, sub, tq=ATTN_Q_TILE)
    x5 = _proj_add(x4, o_f.reshape(n_f, d), wo_b, tm=FRAME_TILE)
    yf = _ffn(x5, *ffn_w[1][1], tm=FRAME_TILE, final_g=fin)

    y_prompt = yf.reshape(bp, tp, d)
    y_sample = ys[N_META:].reshape(bd, ts, d)
    wkv_p = _head_states(st_f)[:, None]
    wkv_s = _head_states(st_s[1:])[:, None]
    last_tile = tp // RWKV_TILE - 1
    shift_p = hl_f.reshape(bp, tp // RWKV_TILE, d)[:, last_tile][:, None]
    shift_s = hl_s[1:, 0][:, None]
    meta_rows = lambda z: z[:N_META].reshape(N_META * N_HEADS, V_DIM)
    new_k_p = _put_rows(k_f, meta_rows(k_s)).reshape(bp, N_META + tp, N_HEADS, V_DIM)
    new_v_p = _put_rows(v_f, meta_rows(v_s)).reshape(bp, N_META + tp, N_HEADS, V_DIM)
    new_k_s = k_s[N_META:].reshape(bd, ts, N_HEADS, V_DIM)
    new_v_s = v_s[N_META:].reshape(bd, ts, N_HEADS, V_DIM)
    return (y_prompt, y_sample, wkv_p, shift_p, new_k_p, new_v_p, wkv_s, shift_s, new_k_s, new_v_s)
```

```python
import functools
import math

import jax
import jax.numpy as jnp
import numpy as np
from jax import lax
from jax.experimental import pallas as pl
from jax.experimental.pallas import tpu as pltpu

f32 = jnp.float32
bf16 = jnp.bfloat16

D_MODEL = 1024
CHUNK = 64
N_META = 16
RW_HEAD = 64
RW_HEADS = D_MODEL // RW_HEAD
GN_EPS = 64e-5
HEAD_DIM = 64
N_HEADS = D_MODEL // (2 * HEAD_DIM)
V_DIM = 2 * HEAD_DIM
ROPE_DIM = HEAD_DIM // 4
ROPE_THETA = 500000.0
SUBLN_EPS = 1e-5
NEG_INF = -1e30
NORM_EPS = 1e-6
LAM_INIT = 0.8 - 0.6 * math.exp(-0.3 * 1)
LOG2E = math.log2(math.e)
CHUNK_SHIFT = CHUNK.bit_length() - 1

LANES = 128
HEADS_PER_VREG = LANES // RW_HEAD
N_PAIRS = D_MODEL // LANES
VMEM_LIMIT = 56 * 1024 * 1024


def _cparams(sem):
    return pltpu.CompilerParams(dimension_semantics=sem, vmem_limit_bytes=VMEM_LIMIT)


def _dot(a, b):
    return jnp.dot(a, b, preferred_element_type=f32)


def _dot_nt(a, b):
    return lax.dot_general(a, b, (((1,), (1,)), ((), ())), preferred_element_type=f32)


def _dot_tn(a, b):
    return lax.dot_general(a, b, (((0,), (0,)), ((), ())), preferred_element_type=f32)


def _split2(x):
    hi = x.astype(bf16)
    lo = (x - hi.astype(f32)).astype(bf16)
    return hi, lo


def _split3(x):
    h1 = x.astype(bf16)
    r1 = x - h1.astype(f32)
    h2 = r1.astype(bf16)
    h3 = (r1 - h2.astype(f32)).astype(bf16)
    return h1, h2, h3


def _sigmoid(x):
    return 1.0 / (1.0 + jnp.exp(-x))


def _rms(x, g, eps):
    return x * lax.rsqrt(jnp.mean(x * x, axis=-1, keepdims=True) + eps) * g


def _headsum(x, g1, g2):
    hi, lo = _split2(x)
    s = _dot(hi, g1) + _dot(lo, g1)
    shi, slo = _split2(s)
    return _dot(shi, g2) + _dot(slo, g2)


def _rope(x, cos, sin_a, sin_b):
    reps = x.shape[1] // LANES
    c = jnp.concatenate([cos] * reps, axis=1)
    sa = jnp.concatenate([sin_a] * reps, axis=1)
    sb = jnp.concatenate([sin_b] * reps, axis=1)
    half = ROPE_DIM // 2
    return x * c + pltpu.roll(x, x.shape[1] - half, 1) * sa + pltpu.roll(x, half, 1) * sb


def _row(i):
    return (i, 0)


def _fixed2(i):
    return (0, 0)


def _rows(tm, w):
    return pl.BlockSpec((tm, w), _row)


def _whole(shape):
    return pl.BlockSpec(shape, _fixed2, pipeline_mode=pl.Buffered(1))


def _ffn_kernel(x_ref, g_ref, wg_ref, wu_ref, wd_ref, *rest, final_norm):
    o_ref = rest[-1]
    x = x_ref[...]
    xb = _rms(x, g_ref[...], NORM_EPS).astype(bf16)
    gate = _dot(xb, wg_ref[...])
    up = _dot(xb, wu_ref[...])
    act = (gate * _sigmoid(gate) * up).astype(bf16)
    y = x + 0.5 * _dot(act, wd_ref[...])
    if final_norm:
        y = _rms(y, rest[0][...], NORM_EPS)
    o_ref[...] = y


def _ffn(x, g, wg, wu, wd, layer, half, tm, final_g=None):
    n, d = x.shape
    dff = wg.shape[-1]
    ins = [x, g, wg, wu, wd]

    def pick(rows, cols):
        return pl.BlockSpec((None, None, rows, cols), lambda i: (layer, half, 0, 0), pipeline_mode=pl.Buffered(1))

    specs = [_rows(tm, d), _whole((1, d)), pick(d, dff), pick(d, dff), pick(dff, d)]
    if final_g is not None:
        ins.append(final_g)
        specs.append(_whole((1, d)))
    return pl.pallas_call(
        functools.partial(_ffn_kernel, final_norm=final_g is not None),
        grid=(n // tm,),
        in_specs=specs,
        out_specs=_rows(tm, d),
        out_shape=jax.ShapeDtypeStruct((n, d), f32),
        compiler_params=_cparams(("parallel",)),
        name="ffn_half",
    )(*ins)


def _rwkv_proj_kernel(x_ref, xprev_ref, shift_ref, nm_ref, mu_ref, wr_ref, wk_ref, wv_ref,
                      w0_ref, w1_ref, w2_ref, a0_ref, a1_ref, a2_ref, g1_ref, g2_ref,
                      kk_ref, ka_ref, hs1_ref, hs2_ref,
                      r_ref, lw_ref, k_ref, v_ref, as_ref, bs_ref, g_ref, hlast_ref,
                      h_scr, *, tm, tiles_per_seq):
    nm = nm_ref[...]
    h = _rms(x_ref[...], nm, NORM_EPS)
    hlast_ref[0] = h[tm - 1:tm, :]
    first = (pl.program_id(0) % tiles_per_seq) == 0
    h_before = _rms(xprev_ref[7:8, :], nm, NORM_EPS)
    h_scr[8:tm + 8, :] = h
    h_scr[7:8, :] = jnp.where(first, shift_ref[0], h_before)
    xx = h_scr[7:tm + 7, :] - h
    mu = mu_ref[...]

    def mix(m):
        return (h + xx * mu[m:m + 1, :]).astype(bf16)

    r = _dot(mix(0), wr_ref[...])
    k = _dot(mix(2), wk_ref[...])
    v = _dot(mix(3), wv_ref[...])
    ww = w0_ref[...] + _dot(jnp.tanh(_dot(mix(1), w1_ref[...])).astype(bf16), w2_ref[...])
    neg = -ww
    softplus = jnp.maximum(neg, 0.0) + jnp.log(1.0 + jnp.exp(-jnp.abs(neg)))
    lw = -jnp.exp(-softplus - 0.5)
    a = _sigmoid(a0_ref[...] + _dot(_dot(mix(4), a1_ref[...]).astype(bf16), a2_ref[...]))
    g = _dot(_sigmoid(_dot(mix(5), g1_ref[...])).astype(bf16), g2_ref[...])
    kk = k * kk_ref[...]
    norm = jnp.sqrt(_headsum(kk * kk, hs1_ref[...], hs2_ref[...]))
    kk = kk / jnp.maximum(norm, 1e-12)
    r_ref[...] = r
    lw_ref[...] = lw
    k_ref[...] = k * (1.0 + (a - 1.0) * ka_ref[...])
    v_ref[...] = v
    as_ref[...] = -kk
    bs_ref[...] = kk * a
    g_ref[...] = g


def _rwkv_proj(x, shift0, p, tm, tiles_per_seq):
    n, d = x.shape
    nt = n // tm
    row = _rows(tm, d)
    lw_, la_, lg_ = p["w1"].shape[1], p["a1"].shape[1], p["g1"].shape[1]
    specs = [
        row,
        pl.BlockSpec((8, d), lambda i: (jnp.maximum(i * (tm // 8) - 1, 0), 0)),
        pl.BlockSpec((1, 1, d), lambda i: (i // tiles_per_seq, 0, 0)),
        _whole((1, d)), _whole((6, d)),
        _whole((d, d)), _whole((d, d)), _whole((d, d)),
        _whole((1, d)), _whole((d, lw_)), _whole((lw_, d)),
        _whole((1, d)), _whole((d, la_)), _whole((la_, d)),
        _whole((d, lg_)), _whole((lg_, d)),
        _whole((1, d)), _whole((1, d)), _whole((d, LANES)), _whole((LANES, d)),
    ]
    outs = [jax.ShapeDtypeStruct((n, d), f32)] * 7 + [jax.ShapeDtypeStruct((nt, 1, d), f32)]
    out_specs = [row] * 7 + [pl.BlockSpec((1, 1, d), lambda i: (i, 0, 0))]
    return pl.pallas_call(
        functools.partial(_rwkv_proj_kernel, tm=tm, tiles_per_seq=tiles_per_seq),
        grid=(nt,),
        in_specs=specs,
        out_specs=out_specs,
        out_shape=outs,
        scratch_shapes=[pltpu.VMEM((tm + 8, d), f32)],
        compiler_params=_cparams(("parallel",)),
        name="rwkv_proj",
    )(x, x, shift0, p["norm_mix"], p["mu"], p["w_r"], p["w_k"], p["w_v"],
      p["w0"], p["w1"], p["w2"], p["a0"], p["a1"], p["a2"], p["g1"], p["g2"],
      p["k_k"], p["k_a"], p["hs1"], p["hs2"])


def _wkv_kernel(r_ref, lw_ref, k_ref, v_ref, a_ref, b_ref, s0_ref, y_ref, sT_ref, s_scr, *, L, nb):
    c = pl.program_id(1)
    seqs = range(nb)
    units = range(nb * N_PAIRS)

    @pl.when(c == 0)
    def _():
        zero = jnp.zeros((RW_HEAD, RW_HEAD), f32)
        for u in units:
            b, p = divmod(u, N_PAIRS)
            top = jnp.concatenate([s0_ref[b, HEADS_PER_VREG * p], zero], axis=1)
            bottom = jnp.concatenate([zero, s0_ref[b, HEADS_PER_VREG * p + 1]], axis=1)
            s_scr[u] = jnp.concatenate([top, bottom], axis=0).T

    L2 = 2 * L
    ri = lax.broadcasted_iota(jnp.int32, (L, L), 0)
    ci = lax.broadcasted_iota(jnp.int32, (L, L), 1)
    cum = jnp.where(ri >= ci, 1.0, 0.0).astype(bf16)
    rs = lax.broadcasted_iota(jnp.int32, (L2, LANES), 0)
    ls = lax.broadcasted_iota(jnp.int32, (L2, LANES), 1)
    own = (rs < L) == (ls < RW_HEAD)
    rg = lax.broadcasted_iota(jnp.int32, (L2, L2), 0) & (L - 1)
    cg = lax.broadcasted_iota(jnp.int32, (L2, L2), 1) & (L - 1)
    strict = rg > cg
    incl = rg >= cg

    lw = [lw_ref[b] for b in seqs]
    parts = [_split3(lw[b]) for b in seqs]
    cs = [_dot(cum, parts[b][0]) + _dot(cum, parts[b][1]) + _dot(cum, parts[b][2]) for b in seqs]
    w_in = [jnp.exp(cs[b]) for b in seqs]
    w_inv = [jnp.exp(-cs[b]) for b in seqs]
    pairs = units

    def stack(xs):
        out = []
        for u in units:
            b, p = divmod(u, N_PAIRS)
            xp = xs[b][:, p * LANES:(p + 1) * LANES]
            out.append(jnp.where(own, jnp.concatenate([xp, xp], axis=0), 0.0).astype(bf16))
        return out

    a_st = stack([a_ref[b] * jnp.exp(cs[b] - lw[b]) for b in seqs])
    r_st = stack([r_ref[b] * w_in[b] for b in seqs])
    b_st = stack([b_ref[b] * w_inv[b] for b in seqs])
    k_st = stack([k_ref[b] * w_inv[b] for b in seqs])
    v_st = stack([v_ref[b] for b in seqs])
    gram = [_dot_nt(jnp.concatenate([a_st[p], r_st[p]], axis=0), jnp.concatenate([b_st[p], k_st[p]], axis=0))
            for p in pairs]
    m = [jnp.where(strict, gram[p][:L2, :L2], 0.0) for p in pairs]
    a_ak = [jnp.where(strict, gram[p][:L2, L2:], 0.0).astype(bf16) for p in pairs]
    a_rb = [jnp.where(incl, gram[p][L2:, :L2], 0.0).astype(bf16) for p in pairs]
    a_rk = [jnp.where(incl, gram[p][L2:, L2:], 0.0).astype(bf16) for p in pairs]
    s = [s_scr[p] for p in pairs]
    sb = [s[p].astype(bf16) for p in pairs]

    rhs = [_dot(jnp.concatenate([a_st[p], a_ak[p]], axis=1), jnp.concatenate([sb[p], v_st[p]], axis=0))
           for p in pairs]
    t = m
    mb = [m[p].astype(bf16) for p in pairs]
    m = [_dot(mb[p], mb[p]) for p in pairs]
    n_sq = L.bit_length() - 2
    for k in range(1, n_sq + 1):
        mb = [m[p].astype(bf16) for p in pairs]
        if k < n_sq:
            both = [_dot(mb[p], jnp.concatenate([t[p].astype(bf16), mb[p]], axis=1)) for p in pairs]
            t = [t[p] + m[p] + both[p][:, :L2] for p in pairs]
            m = [both[p][:, L2:] for p in pairs]
        else:
            t = [t[p] + m[p] + _dot(mb[p], t[p].astype(bf16)) for p in pairs]
    ub = [(rhs[p] + _dot(t[p].astype(bf16), rhs[p].astype(bf16))).astype(bf16) for p in pairs]
    y_st = [_dot(jnp.concatenate([r_st[p], a_rb[p], a_rk[p]], axis=1),
                 jnp.concatenate([sb[p], ub[p], v_st[p]], axis=0)) for p in pairs]
    for b in seqs:
        y_ref[b] = jnp.concatenate([y_st[u][:L] + y_st[u][L:] for u in units[b * N_PAIRS:(b + 1) * N_PAIRS]], axis=1)
    eye = (lax.broadcasted_iota(jnp.int32, (LANES, LANES), 0)
           == lax.broadcasted_iota(jnp.int32, (LANES, LANES), 1)).astype(f32)
    for u in units:
        b, p = divmod(u, N_PAIRS)
        w_rows = jnp.sum(eye * w_in[b][L - 1:L, p * LANES:(p + 1) * LANES], axis=1, keepdims=True)
        s_scr[u] = (s[u] + _dot_tn(jnp.concatenate([b_st[u], k_st[u]], axis=0),
                                   jnp.concatenate([ub[u], v_st[u]], axis=0))) * w_rows

    @pl.when(c == pl.num_programs(1) - 1)
    def _():
        for u in units:
            b, p = divmod(u, N_PAIRS)
            s_vk = s_scr[u].T
            sT_ref[b, HEADS_PER_VREG * p] = s_vk[:RW_HEAD, :RW_HEAD]
            sT_ref[b, HEADS_PER_VREG * p + 1] = s_vk[RW_HEAD:, RW_HEAD:]


def _wkv(r, lw, k, v, a, b, s0, L, nb):
    nseq, t, d = r.shape
    blk = pl.BlockSpec((nb, L, d), lambda s, c: (s, c, 0))
    st = pl.BlockSpec((nb, RW_HEADS, RW_HEAD, RW_HEAD), lambda s, c: (s, 0, 0, 0))
    return pl.pallas_call(
        functools.partial(_wkv_kernel, L=L, nb=nb),
        grid=(nseq // nb, t // L),
        in_specs=[blk] * 6 + [st],
        out_specs=[blk, st],
        out_shape=[jax.ShapeDtypeStruct((nseq, t, d), f32),
                   jax.ShapeDtypeStruct((nseq, RW_HEADS, RW_HEAD, RW_HEAD), f32)],
        scratch_shapes=[pltpu.VMEM((nb * N_PAIRS, LANES, LANES), f32)],
        compiler_params=_cparams(("parallel", "arbitrary")),
        name="wkv_scan",
    )(r, lw, k, v, a, b, s0)


def _rwkv_out_kernel(x_ref, y_ref, r_ref, k_ref, v_ref, g_ref, rk_ref, gg_ref, gb_ref,
                     wo_ref, hs1_ref, hs2_ref, o_ref):
    hs1 = hs1_ref[...]
    hs2 = hs2_ref[...]
    y = y_ref[...]
    d = y - _headsum(y, hs1, hs2) * (1.0 / RW_HEAD)
    var = _headsum(d * d, hs1, hs2) * (1.0 / RW_HEAD)
    yn = d * lax.rsqrt(var + GN_EPS) * gg_ref[...] + gb_ref[...]
    v = v_ref[...]
    bonus = _headsum(r_ref[...] * k_ref[...] * rk_ref[...], hs1, hs2) * v
    o_ref[...] = x_ref[...] + _dot(((yn + bonus) * g_ref[...]).astype(bf16), wo_ref[...])


def _rwkv_out(x, y, r, k, v, g, p, tm):
    n, d = x.shape
    row = _rows(tm, d)
    return pl.pallas_call(
        _rwkv_out_kernel,
        grid=(n // tm,),
        in_specs=[row] * 6 + [_whole((1, d))] * 3 + [_whole((d, d)), _whole((d, LANES)), _whole((LANES, d))],
        out_specs=row,
        out_shape=jax.ShapeDtypeStruct((n, d), f32),
        compiler_params=_cparams(("parallel",)),
        name="rwkv_out",
    )(x, y, r, k, v, g, p["r_k"], p["gn_g"], p["gn_b"], p["w_o"], p["hs1"], p["hs2"])


def _kv_proj_kernel(x_ref, g_ref, wk_ref, wv_ref, cos_ref, sa_ref, sb_ref, k_ref, v_ref, kb_ref, vb_ref,
                    *, cache_layout):
    hb = _rms(x_ref[...], g_ref[...], NORM_EPS).astype(bf16)
    k = _rope(_dot(hb, wk_ref[...]), cos_ref[...], sa_ref[...], sb_ref[...])
    v = _dot(hb, wv_ref[...])
    kb_ref[...] = k.astype(bf16)
    vb_ref[...] = v.astype(bf16)
    if cache_layout:
        tm = k.shape[0]
        for h in range(N_HEADS):
            k_ref[0, pl.ds(h, tm, stride=N_HEADS), :] = k[:, h * V_DIM:(h + 1) * V_DIM]
            v_ref[0, pl.ds(h, tm, stride=N_HEADS), :] = v[:, h * V_DIM:(h + 1) * V_DIM]
    else:
        k_ref[...] = k
        v_ref[...] = v


def _kv_proj(x, g, wk, wv, rope_tabs, tm, tab_blocks, cache_rows=None):
    n, d = x.shape
    row = _rows(tm, d)
    tab = pl.BlockSpec((tm, LANES), lambda i: (i % tab_blocks, 0))
    if cache_rows is None:
        f32_spec = row
        f32_shape = jax.ShapeDtypeStruct((n, d), f32)
    else:
        n_seq, rows_per_seq, first_row = cache_rows
        tiles_per_seq = n // (n_seq * tm)
        f32_spec = pl.BlockSpec(
            (pl.Element(1), pl.Element(tm * N_HEADS), pl.Element(V_DIM)),
            lambda i: (i // tiles_per_seq, (first_row + (i % tiles_per_seq) * tm) * N_HEADS, 0))
        f32_shape = jax.ShapeDtypeStruct((n_seq, rows_per_seq * N_HEADS, V_DIM), f32)
    return pl.pallas_call(
        functools.partial(_kv_proj_kernel, cache_layout=cache_rows is not None),
        grid=(n // tm,),
        in_specs=[row, _whole((1, d)), _whole((d, d)), _whole((d, d)), tab, tab, tab],
        out_specs=[f32_spec, f32_spec, row, row],
        out_shape=[f32_shape] * 2 + [jax.ShapeDtypeStruct((n, d), bf16)] * 2,
        compiler_params=_cparams(("parallel",)),
        name="kv_proj",
    )(x, g, wk, wv, *rope_tabs)


def _put_rows_kernel(big_ref, rows_ref, o_ref):
    del big_ref
    o_ref[0] = rows_ref[...]


def _put_rows(big, rows):
    n_seq = big.shape[0]
    r = rows.shape[0]
    return pl.pallas_call(
        _put_rows_kernel,
        grid=(n_seq,),
        in_specs=[pl.BlockSpec(memory_space=pl.ANY), pl.BlockSpec((r, LANES), lambda s: (0, 0))],
        out_specs=pl.BlockSpec((1, r, LANES), lambda s: (s, 0, 0)),
        out_shape=jax.ShapeDtypeStruct(big.shape, big.dtype),
        input_output_aliases={0: 0},
        compiler_params=_cparams(("parallel",)),
        name="put_meta_rows",
    )(big, rows)


def _q_proj_kernel(x_ref, g_ref, wq_ref, cos_ref, sa_ref, sb_ref, q_ref):
    hb = _rms(x_ref[...], g_ref[...], NORM_EPS).astype(bf16)
    q = _rope(_dot(hb, wq_ref[...]), cos_ref[...], sa_ref[...], sb_ref[...])
    q_ref[...] = (q * (HEAD_DIM ** -0.5 * LOG2E)).astype(bf16)


def _q_proj(x, g, wq, rope_tabs, tm, tab_blocks):
    n, d = x.shape
    row = _rows(tm, d)
    tab = pl.BlockSpec((tm, LANES), lambda i: (i % tab_blocks, 0))
    return pl.pallas_call(
        _q_proj_kernel,
        grid=(n // tm,),
        in_specs=[row, _whole((1, d)), _whole((d, d)), tab, tab, tab],
        out_specs=row,
        out_shape=jax.ShapeDtypeStruct((n, d), bf16),
        compiler_params=_cparams(("parallel",)),
        name="q_proj",
    )(x, g, wq, *rope_tabs)


def _lam(lp):
    e1 = jnp.exp(jnp.sum(lp[0:1, :] * lp[1:2, :], axis=-1, keepdims=True))
    e2 = jnp.exp(jnp.sum(lp[2:3, :] * lp[3:4, :], axis=-1, keepdims=True))
    return e1 - e2 + LAM_INIT


def _split_heads(q):
    lane = lax.broadcasted_iota(jnp.int32, q.shape, 1)
    zero = jnp.zeros_like(q)
    return jnp.where(lane < HEAD_DIM, q, zero), jnp.where(lane >= HEAD_DIM, q, zero)


def _stack_components(q):
    q1, q2 = _split_heads(q)
    return jnp.concatenate([q1, q2], axis=0)


def _with_ones(v):
    return jnp.concatenate([v, jnp.ones(v.shape, v.dtype)], axis=1)


def _attn_finish(acc, lp, sub):
    n = acc.shape[0] // 2
    o = acc[:n, :V_DIM] / acc[:n, V_DIM:] - _lam(lp) * (acc[n:, :V_DIM] / acc[n:, V_DIM:])
    o = o * lax.rsqrt(jnp.mean(o * o, axis=-1, keepdims=True) + SUBLN_EPS) * sub
    return o * (1.0 - LAM_INIT)


def _attn_prompt_kernel(q_ref, k_ref, v_ref, km_ref, vm_ref, bias_ref, lp_ref, sub_ref, o_ref, *, tq):
    i = pl.program_id(2)
    qs = _split_heads(q_ref[0])
    comps = range(len(qs))

    def tile(j, carry, diagonal):
        start = pl.multiple_of(j * tq, tq)
        kj = k_ref[0, pl.ds(start, tq), :]
        vj = v_ref[0, pl.ds(start, tq), :]
        s = [_dot_nt(qs[c], kj) for c in comps]
        if diagonal:
            bias = bias_ref[...]
            s = [s[c] + bias for c in comps]
        if diagonal:
            sm = [_dot_nt(qs[c], km_ref[...]) for c in comps]
        m_blk = [jnp.max(s[c], axis=-1, keepdims=True) for c in comps]
        if diagonal:
            m_blk = [jnp.maximum(m_blk[c], jnp.max(sm[c], axis=-1, keepdims=True)) for c in comps]
        m_new = [jnp.maximum(carry[3 * c], m_blk[c]) for c in comps]
        alpha = [jnp.exp2(carry[3 * c] - m_new[c]) for c in comps]
        p = [jnp.exp2(s[c] - m_new[c]) for c in comps]
        l_new = [alpha[c] * carry[3 * c + 1] + jnp.sum(p[c], axis=-1, keepdims=True) for c in comps]
        acc = [alpha[c] * carry[3 * c + 2] + _dot(p[c].astype(bf16), vj) for c in comps]
        if diagonal:
            pm = [jnp.exp2(sm[c] - m_new[c]) for c in comps]
            l_new = [l_new[c] + jnp.sum(pm[c], axis=-1, keepdims=True) for c in comps]
            acc = [acc[c] + _dot(pm[c].astype(bf16), vm_ref[...]) for c in comps]
        out = []
        for c in comps:
            out += [m_new[c], l_new[c], acc[c]]
        return tuple(out)

    init = []
    for _ in comps:
        init += [jnp.full((tq, 1), NEG_INF, f32), jnp.zeros((tq, 1), f32), jnp.zeros((tq, V_DIM), f32)]
    carry = lax.fori_loop(0, i, lambda j, c: tile(j, c, False), tuple(init))
    carry = tile(i, carry, True)
    acc = jnp.concatenate([jnp.concatenate([carry[3 * c + 2], jnp.broadcast_to(carry[3 * c + 1], (tq, V_DIM))], axis=1)
                           for c in comps], axis=0)
    o_ref[0] = _attn_finish(acc, lp_ref[...], sub_ref[...]).astype(o_ref.dtype)


def _attn_prompt(q, k, v, km, vm, lp, sub, tq):
    b, t, d = q.shape
    qspec = pl.BlockSpec((1, tq, LANES), lambda bi, h, i: (bi, i, h))
    kspec = pl.BlockSpec((1, t, LANES), lambda bi, h, i: (bi, 0, h))
    mspec = pl.BlockSpec((N_META, LANES), lambda bi, h, i: (0, h))
    chunk_of = jnp.arange(tq) // CHUNK
    bias = jnp.where(chunk_of[None, :] <= chunk_of[:, None], 0.0, NEG_INF).astype(f32)
    return pl.pallas_call(
        functools.partial(_attn_prompt_kernel, tq=tq),
        grid=(b, N_HEADS, t // tq),
        in_specs=[qspec, kspec, kspec, mspec, mspec,
                  pl.BlockSpec((tq, tq), lambda bi, h, i: (0, 0), pipeline_mode=pl.Buffered(1)),
                  pl.BlockSpec((4, HEAD_DIM), lambda bi, h, i: (0, 0)),
                  pl.BlockSpec((1, V_DIM), lambda bi, h, i: (0, 0))],
        out_specs=qspec,
        out_shape=jax.ShapeDtypeStruct((b, t, d), bf16),
        compiler_params=_cparams(("parallel", "parallel", "arbitrary")),
        name="attn_prompt",
    )(q, k, v, km, vm, bias, lp, sub)


def _attn_full_kernel(*refs, n_cache):
    if n_cache:
        q_ref, kc_ref, vc_ref, kn_ref, vn_ref, lp_ref, sub_ref, o_ref = refs
    else:
        q_ref, kn_ref, vn_ref, lp_ref, sub_ref, o_ref = refs
    lp = lp_ref[...]
    sub = sub_ref[...]
    for h in range(N_HEADS):
        cols = slice(h * V_DIM, (h + 1) * V_DIM)
        qq = _stack_components(q_ref[0, :, cols])
        keys = [kn_ref[0, :, cols]]
        vals = [vn_ref[0, :, cols]]
        if n_cache:
            for first, count in ((0, N_META), (N_META, n_cache - N_META)):
                rows = pl.ds(first * N_HEADS + h, count, stride=N_HEADS)
                keys.append(kc_ref[0, rows, :].astype(bf16))
                vals.append(vc_ref[0, rows, :].astype(bf16))
        scores = [_dot_nt(qq, kk) for kk in keys]
        m = functools.reduce(jnp.maximum, [jnp.max(s, axis=-1, keepdims=True) for s in scores])
        acc = sum(_dot(jnp.exp2(s - m).astype(bf16), _with_ones(vv)) for s, vv in zip(scores, vals))
        o_ref[0, :, cols] = _attn_finish(acc, lp, sub).astype(o_ref.dtype)


def _attn_full(q, kn, vn, lp, sub, cache=None):
    b, tq, d = q.shape
    spec = pl.BlockSpec((1, tq, d), lambda bi: (bi, 0, 0))
    ins, specs, n_cache = [q], [spec], 0
    if cache is not None:
        n_cache = cache[0].shape[1] // N_HEADS
        cspec = pl.BlockSpec((1, n_cache * N_HEADS, V_DIM), lambda bi: (bi, 0, 0))
        ins += list(cache)
        specs += [cspec, cspec]
    ins += [kn, vn, lp, sub]
    specs += [spec, spec, pl.BlockSpec((4, HEAD_DIM), lambda bi: (0, 0)),
              pl.BlockSpec((1, V_DIM), lambda bi: (0, 0))]
    return pl.pallas_call(
        functools.partial(_attn_full_kernel, n_cache=n_cache),
        grid=(b,),
        in_specs=specs,
        out_specs=spec,
        out_shape=jax.ShapeDtypeStruct((b, tq, d), bf16),
        compiler_params=_cparams(("parallel",)),
        name="attn_full",
    )(*ins)


def _proj_add_kernel(x_ref, o_ref_in, w_ref, y_ref):
    y_ref[...] = x_ref[...] + _dot(o_ref_in[...], w_ref[...])


def _proj_add(x, o, w, tm):
    n, d = x.shape
    row = _rows(tm, d)
    return pl.pallas_call(
        _proj_add_kernel,
        grid=(n // tm,),
        in_specs=[row, row, _whole((d, d))],
        out_specs=row,
        out_shape=jax.ShapeDtypeStruct((n, d), f32),
        compiler_params=_cparams(("parallel",)),
        name="attn_out_proj",
    )(x, o, w)


FRAME_TILE = 512
RWKV_TILE = 256
ATTN_Q_TILE = 512
WKV_CHUNK = 64
WKV_SEQS_PER_STEP = 2


def _rope_tables(pos):
    half = ROPE_DIM // 2
    inv = np.power(np.float64(ROPE_THETA), -np.arange(0, ROPE_DIM, 2, dtype=np.float64) / ROPE_DIM)
    ang = pos.astype(np.float64)[:, None] * inv[None, :]
    cos, sin = np.cos(ang).astype(np.float32), np.sin(ang).astype(np.float32)
    n = pos.shape[0]
    one = np.ones((n, HEAD_DIM - ROPE_DIM), np.float32)
    zero = np.zeros((n, HEAD_DIM - ROPE_DIM), np.float32)
    zh = np.zeros((n, half), np.float32)
    c = np.concatenate([cos, cos, one], axis=1)
    sa = np.concatenate([-sin, zh, zero], axis=1)
    sb = np.concatenate([zh, sin, zero], axis=1)
    return tuple(jnp.asarray(np.concatenate([t, t], axis=1), dtype=f32) for t in (c, sa, sb))


def kernel(x_prompt, x_sample, cache_k, cache_v, state_wkv, state_shift, meta_tokens, norm_ffn, norm_mix, ffn_w_gate, ffn_w_up, ffn_w_down, rw_mu, rw_w_r, rw_w_k, rw_w_v, rw_w_o, rw_w0, rw_w1, rw_w2, rw_a0, rw_a1, rw_a2, rw_g1, rw_g2, rw_k_k, rw_k_a, rw_r_k, rw_gn_g, rw_gn_b, kv_norm, w_k, w_v, b_w_q, b_w_o, b_lambda, b_subln, final_norm):
    bp, tp, d = x_prompt.shape
    bd, ts, _ = x_sample.shape
    past = cache_k.shape[1] - N_META
    n_small = N_META + bd * ts
    vec = lambda a: a.reshape(1, -1).astype(f32)
    wb = lambda a: a.astype(bf16)

    head_of_lane = jnp.arange(d) // RW_HEAD
    hs1 = (head_of_lane[:, None] == jnp.arange(LANES)[None, :]).astype(bf16)
    hs2 = hs1.T
    rw = dict(norm_mix=vec(norm_mix[0]), mu=rw_mu[0], w_r=wb(rw_w_r[0]), w_k=wb(rw_w_k[0]), w_v=wb(rw_w_v[0]),
              w_o=wb(rw_w_o[0]), w0=vec(rw_w0[0]), w1=wb(rw_w1[0]), w2=wb(rw_w2[0]), a0=vec(rw_a0[0]),
              a1=wb(rw_a1[0]), a2=wb(rw_a2[0]), g1=wb(rw_g1[0]), g2=wb(rw_g2[0]), k_k=vec(rw_k_k[0]),
              k_a=vec(rw_k_a[0]), r_k=vec(rw_r_k[0]), gn_g=vec(rw_gn_g[0]), gn_b=vec(rw_gn_b[0]),
              hs1=hs1, hs2=hs2)
    wg_all, wu_all, wd_all = wb(ffn_w_gate), wb(ffn_w_up), wb(ffn_w_down)
    ffn_half = lambda x, li, j, **kw: _ffn(x, vec(norm_ffn[li, j]), wg_all, wu_all, wd_all, li, j, **kw)
    wk_b, wv_b, wq_b, wo_b = wb(w_k), wb(w_v), wb(b_w_q[0]), wb(b_w_o[0])
    lp = b_lambda[0].astype(f32)
    sub = vec(b_subln[0])
    fin = vec(final_norm)

    seq = N_META
    xs = jnp.concatenate([meta_tokens.astype(f32), x_sample.reshape(bd * ts, d)], axis=0)
    n_seq_s = n_small // seq
    shift_s0 = jnp.concatenate([jnp.zeros((1, d), f32), state_shift[:, 0]], axis=0)[:, None, :]
    wkv_s0 = jnp.concatenate([jnp.zeros((1,) + state_wkv.shape[2:], f32), state_wkv[:, 0]], axis=0)
    pos_s = np.concatenate([np.arange(N_META), np.tile(N_META + past + np.arange(ts), bd)])
    tabs_s = _rope_tables(pos_s)

    x1s = ffn_half(xs, 0, 0, tm=n_small)
    r, lw, k, v, a_s, b_s, g, hl_s = _rwkv_proj(x1s, shift_s0, rw, tm=seq, tiles_per_seq=1)
    sq = lambda z: z.reshape(n_seq_s, seq, d)
    y, st_s = _wkv(sq(r), sq(lw), sq(k), sq(v), sq(a_s), sq(b_s), wkv_s0, L=seq, nb=1)
    x2s = _rwkv_out(x1s, y.reshape(n_small, d), r, k, v, g, rw, tm=n_small)
    x3s = ffn_half(x2s, 0, 1, tm=n_small)
    k_s, v_s, kb_s, vb_s = _kv_proj(x3s, vec(kv_norm), wk_b, wv_b, tabs_s, tm=n_small, tab_blocks=1)
    x4s = ffn_half(x3s, 1, 0, tm=n_small)
    q_s = _q_proj(x4s, vec(norm_mix[1]), wq_b, tabs_s, tm=n_small, tab_blocks=1)
    o_meta = _attn_full(q_s[None, :N_META], kb_s[None, :N_META], vb_s[None, :N_META], lp, sub)
    bs3 = lambda z: z[N_META:].reshape(bd, ts, d)
    cache_rows = lambda z: z.reshape(bd, (N_META + past) * N_HEADS, V_DIM)
    o_samp = _attn_full(bs3(q_s), bs3(kb_s), bs3(vb_s), lp, sub, cache=(cache_rows(cache_k), cache_rows(cache_v)))
    o_s = jnp.concatenate([o_meta[0], o_samp.reshape(bd * ts, d)], axis=0)
    x5s = _proj_add(x4s, o_s, wo_b, tm=n_small)
    ys = ffn_half(x5s, 1, 1, tm=n_small, final_g=fin)

    n_f = bp * tp
    xf = x_prompt.reshape(n_f, d)
    tabs_f = _rope_tables(N_META + np.arange(tp))
    shift_f0 = jnp.broadcast_to(hl_s[0:1], (bp, 1, d))
    wkv_f0 = jnp.broadcast_to(st_s[0:1], (bp,) + st_s.shape[1:])

    x1 = ffn_half(xf, 0, 0, tm=FRAME_TILE)
    r, lw, k, v, a_s, b_s, g, hl_f = _rwkv_proj(x1, shift_f0, rw, tm=RWKV_TILE, tiles_per_seq=tp // RWKV_TILE)
    sq = lambda z: z.reshape(bp, tp, d)
    y, st_f = _wkv(sq(r), sq(lw), sq(k), sq(v), sq(a_s), sq(b_s), wkv_f0, L=WKV_CHUNK, nb=WKV_SEQS_PER_STEP)
    x2 = _rwkv_out(x1, y.reshape(n_f, d), r, k, v, g, rw, tm=RWKV_TILE)
    x3 = ffn_half(x2, 0, 1, tm=FRAME_TILE)
    k_f, v_f, kb_f, vb_f = _kv_proj(x3, vec(kv_norm), wk_b, wv_b, tabs_f, tm=FRAME_TILE, tab_blocks=tp // FRAME_TILE,
                                    cache_rows=(bp, N_META + tp, N_META))
    x4 = ffn_half(x3, 1, 0, tm=FRAME_TILE)
    q_f = _q_proj(x4, vec(norm_mix[1]), wq_b, tabs_f, tm=FRAME_TILE, tab_blocks=tp // FRAME_TILE)
    o_f = _attn_prompt(sq(q_f), sq(kb_f), sq(vb_f), kb_s[:N_META], vb_s[:N_META], lp, sub, tq=ATTN_Q_TILE)
    x5 = _proj_add(x4, o_f.reshape(n_f, d), wo_b, tm=FRAME_TILE)
    yf = ffn_half(x5, 1, 1, tm=FRAME_TILE, final_g=fin)

    y_prompt = yf.reshape(bp, tp, d)
    y_sample = ys[N_META:].reshape(bd, ts, d)
    wkv_p = st_f[:, None]
    wkv_s = st_s[1:][:, None]
    last_tile = tp // RWKV_TILE - 1
    shift_p = hl_f.reshape(bp, tp // RWKV_TILE, d)[:, last_tile][:, None]
    shift_s = hl_s[1:, 0][:, None]
    meta_rows = lambda z: z[:N_META].reshape(N_META * N_HEADS, V_DIM)
    new_k_p = _put_rows(k_f, meta_rows(k_s)).reshape(bp, N_META + tp, N_HEADS, V_DIM)
    new_v_p = _put_rows(v_f, meta_rows(v_s)).reshape(bp, N_META + tp, N_HEADS, V_DIM)
    new_k_s = k_s[N_META:].reshape(bd, ts, N_HEADS, V_DIM)
    new_v_s = v_s[N_META:].reshape(bd, ts, N_HEADS, V_DIM)
    return (y_prompt, y_sample, wkv_p, shift_p, new_k_p, new_v_p, wkv_s, shift_s, new_k_s, new_v_s)
```

```python
import functools
import math

import jax
import jax.numpy as jnp
import numpy as np
from jax import lax
from jax.experimental import pallas as pl
from jax.experimental.pallas import tpu as pltpu

f32 = jnp.float32
bf16 = jnp.bfloat16

D_MODEL = 1024
CHUNK = 64
N_META = 16
RW_HEAD = 64
RW_HEADS = D_MODEL // RW_HEAD
GN_EPS = 64e-5
HEAD_DIM = 64
N_HEADS = D_MODEL // (2 * HEAD_DIM)
V_DIM = 2 * HEAD_DIM
ROPE_DIM = HEAD_DIM // 4
ROPE_THETA = 500000.0
SUBLN_EPS = 1e-5
NEG_INF = -1e30
NORM_EPS = 1e-6
LAM_INIT = 0.8 - 0.6 * math.exp(-0.3 * 1)
LOG2E = math.log2(math.e)
CHUNK_SHIFT = CHUNK.bit_length() - 1

LANES = 128
HEADS_PER_VREG = LANES // RW_HEAD
N_PAIRS = D_MODEL // LANES
VMEM_LIMIT = 56 * 1024 * 1024


def _cparams(sem):
    return pltpu.CompilerParams(dimension_semantics=sem, vmem_limit_bytes=VMEM_LIMIT)


def _dot(a, b):
    return jnp.dot(a, b, preferred_element_type=f32)


def _dot_nt(a, b):
    return lax.dot_general(a, b, (((1,), (1,)), ((), ())), preferred_element_type=f32)


def _dot_tn(a, b):
    return lax.dot_general(a, b, (((0,), (0,)), ((), ())), preferred_element_type=f32)


def _split2(x):
    hi = x.astype(bf16)
    lo = (x - hi.astype(f32)).astype(bf16)
    return hi, lo


def _split3(x):
    h1 = x.astype(bf16)
    r1 = x - h1.astype(f32)
    h2 = r1.astype(bf16)
    h3 = (r1 - h2.astype(f32)).astype(bf16)
    return h1, h2, h3


def _sigmoid(x):
    return 1.0 / (1.0 + jnp.exp(-x))


def _rms(x, g, eps):
    return x * lax.rsqrt(jnp.mean(x * x, axis=-1, keepdims=True) + eps) * g


def _headsum(x, g1, g2):
    hi, lo = _split2(x)
    s = _dot(hi, g1) + _dot(lo, g1)
    shi, slo = _split2(s)
    return _dot(shi, g2) + _dot(slo, g2)


def _headsum_lanes(x):
    first = lax.broadcasted_iota(jnp.int32, (x.shape[0], LANES), 1) < RW_HEAD
    out = []
    for j in range(x.shape[1] // LANES):
        xs = x[:, j * LANES:(j + 1) * LANES]
        both = jnp.sum(xs, axis=-1, keepdims=True)
        head_a = jnp.sum(jnp.where(first, xs, 0.0), axis=-1, keepdims=True)
        out.append(jnp.where(first, head_a, both - head_a))
    return jnp.concatenate(out, axis=1)


def _rope(x, cos, sin_a, sin_b):
    reps = x.shape[1] // LANES
    c = jnp.concatenate([cos] * reps, axis=1)
    sa = jnp.concatenate([sin_a] * reps, axis=1)
    sb = jnp.concatenate([sin_b] * reps, axis=1)
    half = ROPE_DIM // 2
    return x * c + pltpu.roll(x, x.shape[1] - half, 1) * sa + pltpu.roll(x, half, 1) * sb


def _row(i):
    return (i, 0)


def _fixed2(i):
    return (0, 0)


def _rows(tm, w):
    return pl.BlockSpec((tm, w), _row)


def _whole(shape):
    return pl.BlockSpec(shape, _fixed2, pipeline_mode=pl.Buffered(1))


def _ffn_kernel(*refs, attn_in, final_norm, emit_q, emit_kv, cache_layout):
    it = iter(refs)
    x_ref, g_ref, wg_ref, wu_ref, wd_ref = (next(it) for _ in range(5))
    x = x_ref[...]
    if attn_in:
        o_in_ref, wo_ref = next(it), next(it)
        x = x + _dot(o_in_ref[...], wo_ref[...])
    xb = _rms(x, g_ref[...], NORM_EPS).astype(bf16)
    gate = _dot(xb, wg_ref[...])
    up = _dot(xb, wu_ref[...])
    act = (gate * _sigmoid(gate) * up).astype(bf16)
    y = x + 0.5 * _dot(act, wd_ref[...])
    if final_norm:
        y = _rms(y, next(it)[...], NORM_EPS)
    if emit_q:
        nq_ref, wq_ref, cos_ref, sa_ref, sb_ref = (next(it) for _ in range(5))
    if emit_kv:
        nkv_ref, wk_ref, wv_ref, cos_ref, sa_ref, sb_ref = (next(it) for _ in range(6))
    next(it)[...] = y
    if emit_q:
        hb = _rms(y, nq_ref[...], NORM_EPS).astype(bf16)
        q = _rope(_dot(hb, wq_ref[...]), cos_ref[...], sa_ref[...], sb_ref[...])
        next(it)[...] = (q * (HEAD_DIM ** -0.5 * LOG2E)).astype(bf16)
    if emit_kv:
        k_ref, v_ref, kb_ref, vb_ref = (next(it) for _ in range(4))
        hb = _rms(y, nkv_ref[...], NORM_EPS).astype(bf16)
        k = _rope(_dot(hb, wk_ref[...]), cos_ref[...], sa_ref[...], sb_ref[...])
        v = _dot(hb, wv_ref[...])
        kb_ref[...] = k.astype(bf16)
        vb_ref[...] = v.astype(bf16)
        if cache_layout:
            tm = k.shape[0]
            for h in range(N_HEADS):
                k_ref[0, pl.ds(h, tm, stride=N_HEADS), :] = k[:, h * V_DIM:(h + 1) * V_DIM]
                v_ref[0, pl.ds(h, tm, stride=N_HEADS), :] = v[:, h * V_DIM:(h + 1) * V_DIM]
        else:
            k_ref[...] = k
            v_ref[...] = v


def _ffn(x, g, wg, wu, wd, layer, half, tm, attn=None, final_g=None, q=None, kv=None):
    n, d = x.shape
    dff = wg.shape[-1]
    row = _rows(tm, d)

    def pick(rows, cols):
        return pl.BlockSpec((None, None, rows, cols), lambda i: (layer, half, 0, 0), pipeline_mode=pl.Buffered(1))

    ins = [x, g, wg, wu, wd]
    specs = [row, _whole((1, d)), pick(d, dff), pick(d, dff), pick(dff, d)]
    outs = [jax.ShapeDtypeStruct((n, d), f32)]
    out_specs = [row]
    if attn is not None:
        ins += list(attn)
        specs += [row, _whole((d, d))]
    if final_g is not None:
        ins.append(final_g)
        specs.append(_whole((1, d)))
    cache_rows = None
    if q is not None or kv is not None:
        *weights, tabs, tab_blocks = (q if q is not None else kv[:-1])
        tab = pl.BlockSpec((tm, LANES), lambda i: (i % tab_blocks, 0))
        ins += list(weights) + list(tabs)
        specs += [_whole((1, d))] + [_whole((d, d))] * (len(weights) - 1) + [tab] * 3
    if q is not None:
        outs.append(jax.ShapeDtypeStruct((n, d), bf16))
        out_specs.append(row)
    if kv is not None:
        cache_rows = kv[-1]
        if cache_rows is None:
            f32_spec, f32_shape = row, jax.ShapeDtypeStruct((n, d), f32)
        else:
            n_seq, rows_per_seq, first_row = cache_rows
            tiles_per_seq = n // (n_seq * tm)
            f32_spec = pl.BlockSpec(
                (pl.Element(1), pl.Element(tm * N_HEADS), pl.Element(V_DIM)),
                lambda i: (i // tiles_per_seq, (first_row + (i % tiles_per_seq) * tm) * N_HEADS, 0))
            f32_shape = jax.ShapeDtypeStruct((n_seq, rows_per_seq * N_HEADS, V_DIM), f32)
        outs += [f32_shape] * 2 + [jax.ShapeDtypeStruct((n, d), bf16)] * 2
        out_specs += [f32_spec, f32_spec, row, row]
    return pl.pallas_call(
        functools.partial(_ffn_kernel, attn_in=attn is not None, final_norm=final_g is not None,
                          emit_q=q is not None, emit_kv=kv is not None, cache_layout=cache_rows is not None),
        grid=(n // tm,),
        in_specs=specs,
        out_specs=out_specs,
        out_shape=outs,
        compiler_params=_cparams(("parallel",)),
        name="ffn_half",
    )(*ins)


def _rwkv_proj_kernel(x_ref, xprev_ref, shift_ref, nm_ref, mu_ref, wr_ref, wk_ref, wv_ref,
                      w0_ref, w1_ref, w2_ref, a0_ref, a1_ref, a2_ref, g1_ref, g2_ref,
                      kk_ref, ka_ref,
                      r_ref, lw_ref, k_ref, v_ref, as_ref, bs_ref, g_ref, hlast_ref,
                      h_scr, *, tm, tiles_per_seq):
    nm = nm_ref[...]
    h = _rms(x_ref[...], nm, NORM_EPS)
    hlast_ref[0] = h[tm - 1:tm, :]
    first = (pl.program_id(0) % tiles_per_seq) == 0
    h_before = _rms(xprev_ref[7:8, :], nm, NORM_EPS)
    h_scr[8:tm + 8, :] = h
    h_scr[7:8, :] = jnp.where(first, shift_ref[0], h_before)
    xx = h_scr[7:tm + 7, :] - h
    mu = mu_ref[...]

    def mix(m):
        return (h + xx * mu[m:m + 1, :]).astype(bf16)

    r = _dot(mix(0), wr_ref[...])
    k = _dot(mix(2), wk_ref[...])
    v = _dot(mix(3), wv_ref[...])
    ww = w0_ref[...] + _dot(jnp.tanh(_dot(mix(1), w1_ref[...])).astype(bf16), w2_ref[...])
    neg = -ww
    softplus = jnp.maximum(neg, 0.0) + jnp.log(1.0 + jnp.exp(-jnp.abs(neg)))
    lw = -jnp.exp(-softplus - 0.5)
    a = _sigmoid(a0_ref[...] + _dot(_dot(mix(4), a1_ref[...]).astype(bf16), a2_ref[...]))
    g = _dot(_sigmoid(_dot(mix(5), g1_ref[...])).astype(bf16), g2_ref[...])
    kk = k * kk_ref[...]
    norm = jnp.sqrt(_headsum_lanes(kk * kk))
    kk = kk / jnp.maximum(norm, 1e-12)
    r_ref[...] = r
    lw_ref[...] = lw
    k_ref[...] = k * (1.0 + (a - 1.0) * ka_ref[...])
    v_ref[...] = v
    as_ref[...] = -kk
    bs_ref[...] = kk * a
    g_ref[...] = g


def _rwkv_proj(x, shift0, p, tm, tiles_per_seq):
    n, d = x.shape
    nt = n // tm
    row = _rows(tm, d)
    lw_, la_, lg_ = p["w1"].shape[1], p["a1"].shape[1], p["g1"].shape[1]
    specs = [
        row,
        pl.BlockSpec((8, d), lambda i: (jnp.maximum(i * (tm // 8) - 1, 0), 0)),
        pl.BlockSpec((1, 1, d), lambda i: (i // tiles_per_seq, 0, 0)),
        _whole((1, d)), _whole((6, d)),
        _whole((d, d)), _whole((d, d)), _whole((d, d)),
        _whole((1, d)), _whole((d, lw_)), _whole((lw_, d)),
        _whole((1, d)), _whole((d, la_)), _whole((la_, d)),
        _whole((d, lg_)), _whole((lg_, d)),
        _whole((1, d)), _whole((1, d)),
    ]
    outs = [jax.ShapeDtypeStruct((n, d), f32)] * 7 + [jax.ShapeDtypeStruct((nt, 1, d), f32)]
    out_specs = [row] * 7 + [pl.BlockSpec((1, 1, d), lambda i: (i, 0, 0))]
    return pl.pallas_call(
        functools.partial(_rwkv_proj_kernel, tm=tm, tiles_per_seq=tiles_per_seq),
        grid=(nt,),
        in_specs=specs,
        out_specs=out_specs,
        out_shape=outs,
        scratch_shapes=[pltpu.VMEM((tm + 8, d), f32)],
        compiler_params=_cparams(("parallel",)),
        name="rwkv_proj",
    )(x, x, shift0, p["norm_mix"], p["mu"], p["w_r"], p["w_k"], p["w_v"],
      p["w0"], p["w1"], p["w2"], p["a0"], p["a1"], p["a2"], p["g1"], p["g2"],
      p["k_k"], p["k_a"])


def _wkv_kernel(r_ref, lw_ref, k_ref, v_ref, a_ref, b_ref, s0_ref, y_ref, sT_ref, s_scr, *, L, nb):
    c = pl.program_id(1)
    seqs = range(nb)
    units = range(nb * N_PAIRS)

    @pl.when(c == 0)
    def _():
        zero = jnp.zeros((RW_HEAD, RW_HEAD), f32)
        for u in units:
            b, p = divmod(u, N_PAIRS)
            top = jnp.concatenate([s0_ref[b, HEADS_PER_VREG * p], zero], axis=1)
            bottom = jnp.concatenate([zero, s0_ref[b, HEADS_PER_VREG * p + 1]], axis=1)
            s_scr[u] = jnp.concatenate([top, bottom], axis=0).T

    L2 = 2 * L
    ri = lax.broadcasted_iota(jnp.int32, (L, L), 0)
    ci = lax.broadcasted_iota(jnp.int32, (L, L), 1)
    cum = jnp.where(ri >= ci, 1.0, 0.0).astype(bf16)
    rs = lax.broadcasted_iota(jnp.int32, (L2, LANES), 0)
    ls = lax.broadcasted_iota(jnp.int32, (L2, LANES), 1)
    own = (rs < L) == (ls < RW_HEAD)
    rg = lax.broadcasted_iota(jnp.int32, (L2, L2), 0) & (L - 1)
    cg = lax.broadcasted_iota(jnp.int32, (L2, L2), 1) & (L - 1)
    strict = rg > cg
    incl = rg >= cg

    lw = [lw_ref[b] for b in seqs]
    parts = [_split3(lw[b]) for b in seqs]
    cs = [_dot(cum, parts[b][0]) + _dot(cum, parts[b][1]) + _dot(cum, parts[b][2]) for b in seqs]
    w_in = [jnp.exp(cs[b]) for b in seqs]
    w_inv = [jnp.exp(-cs[b]) for b in seqs]
    pairs = units

    def stack(xs):
        out = []
        for u in units:
            b, p = divmod(u, N_PAIRS)
            xp = xs[b][:, p * LANES:(p + 1) * LANES]
            out.append(jnp.where(own, jnp.concatenate([xp, xp], axis=0), 0.0).astype(bf16))
        return out

    a_st = stack([a_ref[b] * jnp.exp(cs[b] - lw[b]) for b in seqs])
    r_st = stack([r_ref[b] * w_in[b] for b in seqs])
    b_st = stack([b_ref[b] * w_inv[b] for b in seqs])
    k_st = stack([k_ref[b] * w_inv[b] for b in seqs])
    v_st = stack([v_ref[b] for b in seqs])
    gram = [_dot_nt(jnp.concatenate([a_st[p], r_st[p]], axis=0), jnp.concatenate([b_st[p], k_st[p]], axis=0))
            for p in pairs]
    m = [jnp.where(strict, gram[p][:L2, :L2], 0.0) for p in pairs]
    a_ak = [jnp.where(strict, gram[p][:L2, L2:], 0.0).astype(bf16) for p in pairs]
    a_rb = [jnp.where(incl, gram[p][L2:, :L2], 0.0).astype(bf16) for p in pairs]
    a_rk = [jnp.where(incl, gram[p][L2:, L2:], 0.0).astype(bf16) for p in pairs]
    s = [s_scr[p] for p in pairs]
    sb = [s[p].astype(bf16) for p in pairs]

    rhs = [_dot(jnp.concatenate([a_st[p], a_ak[p]], axis=1), jnp.concatenate([sb[p], v_st[p]], axis=0))
           for p in pairs]
    t = m
    mb = [m[p].astype(bf16) for p in pairs]
    m = [_dot(mb[p], mb[p]) for p in pairs]
    n_sq = L.bit_length() - 2
    for k in range(1, n_sq + 1):
        mb = [m[p].astype(bf16) for p in pairs]
        if k < n_sq:
            both = [_dot(mb[p], jnp.concatenate([t[p].astype(bf16), mb[p]], axis=1)) for p in pairs]
            t = [t[p] + m[p] + both[p][:, :L2] for p in pairs]
            m = [both[p][:, L2:] for p in pairs]
        else:
            t = [t[p] + m[p] + _dot(mb[p], t[p].astype(bf16)) for p in pairs]
    ub = [(rhs[p] + _dot(t[p].astype(bf16), rhs[p].astype(bf16))).astype(bf16) for p in pairs]
    y_st = [_dot(jnp.concatenate([r_st[p], a_rb[p], a_rk[p]], axis=1),
                 jnp.concatenate([sb[p], ub[p], v_st[p]], axis=0)) for p in pairs]
    for b in seqs:
        y_ref[b] = jnp.concatenate([y_st[u][:L] + y_st[u][L:] for u in units[b * N_PAIRS:(b + 1) * N_PAIRS]], axis=1)
    eye = (lax.broadcasted_iota(jnp.int32, (LANES, LANES), 0)
           == lax.broadcasted_iota(jnp.int32, (LANES, LANES), 1)).astype(f32)
    for u in units:
        b, p = divmod(u, N_PAIRS)
        w_rows = jnp.sum(eye * w_in[b][L - 1:L, p * LANES:(p + 1) * LANES], axis=1, keepdims=True)
        s_scr[u] = (s[u] + _dot_tn(jnp.concatenate([b_st[u], k_st[u]], axis=0),
                                   jnp.concatenate([ub[u], v_st[u]], axis=0))) * w_rows

    @pl.when(c == pl.num_programs(1) - 1)
    def _():
        for u in units:
            b, p = divmod(u, N_PAIRS)
            s_vk = s_scr[u].T
            sT_ref[b, HEADS_PER_VREG * p] = s_vk[:RW_HEAD, :RW_HEAD]
            sT_ref[b, HEADS_PER_VREG * p + 1] = s_vk[RW_HEAD:, RW_HEAD:]


def _wkv(r, lw, k, v, a, b, s0, L, nb):
    nseq, t, d = r.shape
    blk = pl.BlockSpec((nb, L, d), lambda s, c: (s, c, 0))
    st = pl.BlockSpec((nb, RW_HEADS, RW_HEAD, RW_HEAD), lambda s, c: (s, 0, 0, 0))
    return pl.pallas_call(
        functools.partial(_wkv_kernel, L=L, nb=nb),
        grid=(nseq // nb, t // L),
        in_specs=[blk] * 6 + [st],
        out_specs=[blk, st],
        out_shape=[jax.ShapeDtypeStruct((nseq, t, d), f32),
                   jax.ShapeDtypeStruct((nseq, RW_HEADS, RW_HEAD, RW_HEAD), f32)],
        scratch_shapes=[pltpu.VMEM((nb * N_PAIRS, LANES, LANES), f32)],
        compiler_params=_cparams(("parallel", "arbitrary")),
        name="wkv_scan",
    )(r, lw, k, v, a, b, s0)


def _rwkv_out_kernel(x_ref, y_ref, r_ref, k_ref, v_ref, g_ref, rk_ref, gg_ref, gb_ref,
                     wo_ref, hs1_ref, hs2_ref, o_ref):
    hs1 = hs1_ref[...]
    hs2 = hs2_ref[...]
    y = y_ref[...]
    d = y - _headsum_lanes(y) * (1.0 / RW_HEAD)
    var = _headsum_lanes(d * d) * (1.0 / RW_HEAD)
    yn = d * lax.rsqrt(var + GN_EPS) * gg_ref[...] + gb_ref[...]
    v = v_ref[...]
    bonus = _headsum(r_ref[...] * k_ref[...] * rk_ref[...], hs1, hs2) * v
    o_ref[...] = x_ref[...] + _dot(((yn + bonus) * g_ref[...]).astype(bf16), wo_ref[...])


def _rwkv_out(x, y, r, k, v, g, p, tm):
    n, d = x.shape
    row = _rows(tm, d)
    return pl.pallas_call(
        _rwkv_out_kernel,
        grid=(n // tm,),
        in_specs=[row] * 6 + [_whole((1, d))] * 3 + [_whole((d, d)), _whole((d, LANES)), _whole((LANES, d))],
        out_specs=row,
        out_shape=jax.ShapeDtypeStruct((n, d), f32),
        compiler_params=_cparams(("parallel",)),
        name="rwkv_out",
    )(x, y, r, k, v, g, p["r_k"], p["gn_g"], p["gn_b"], p["w_o"], p["hs1"], p["hs2"])


def _put_rows_kernel(big_ref, rows_ref, o_ref):
    del big_ref
    o_ref[0] = rows_ref[...]


def _put_rows(big, rows):
    n_seq = big.shape[0]
    r = rows.shape[0]
    return pl.pallas_call(
        _put_rows_kernel,
        grid=(n_seq,),
        in_specs=[pl.BlockSpec(memory_space=pl.ANY), pl.BlockSpec((r, LANES), lambda s: (0, 0))],
        out_specs=pl.BlockSpec((1, r, LANES), lambda s: (s, 0, 0)),
        out_shape=jax.ShapeDtypeStruct(big.shape, big.dtype),
        input_output_aliases={0: 0},
        compiler_params=_cparams(("parallel",)),
        name="put_meta_rows",
    )(big, rows)


def _lam(lp):
    e1 = jnp.exp(jnp.sum(lp[0:1, :] * lp[1:2, :], axis=-1, keepdims=True))
    e2 = jnp.exp(jnp.sum(lp[2:3, :] * lp[3:4, :], axis=-1, keepdims=True))
    return e1 - e2 + LAM_INIT


def _split_heads(q):
    lane = lax.broadcasted_iota(jnp.int32, q.shape, 1)
    zero = jnp.zeros_like(q)
    return jnp.where(lane < HEAD_DIM, q, zero), jnp.where(lane >= HEAD_DIM, q, zero)


def _stack_components(q):
    q1, q2 = _split_heads(q)
    return jnp.concatenate([q1, q2], axis=0)


def _with_ones(v):
    return jnp.concatenate([v, jnp.ones(v.shape, v.dtype)], axis=1)


def _attn_finish(acc, lp, sub):
    n = acc.shape[0] // 2
    o = acc[:n, :V_DIM] / acc[:n, V_DIM:] - _lam(lp) * (acc[n:, :V_DIM] / acc[n:, V_DIM:])
    o = o * lax.rsqrt(jnp.mean(o * o, axis=-1, keepdims=True) + SUBLN_EPS) * sub
    return o * (1.0 - LAM_INIT)


def _attn_prompt_kernel(q_ref, k_ref, v_ref, km_ref, vm_ref, bias_ref, lp_ref, sub_ref, o_ref, *, tq, hp):
    i = pl.program_id(2)
    head_cols = [slice(h * V_DIM, (h + 1) * V_DIM) for h in range(hp)]
    qs, head_of = [], []
    for h in range(hp):
        qs += list(_split_heads(q_ref[0, :, head_cols[h]]))
        head_of += [h, h]
    chains = range(len(qs))

    def tile(j, carry, diagonal):
        start = pl.multiple_of(j * tq, tq)
        kj = [k_ref[0, pl.ds(start, tq), head_cols[h]] for h in range(hp)]
        vj = [v_ref[0, pl.ds(start, tq), head_cols[h]] for h in range(hp)]
        s = [_dot_nt(qs[c], kj[head_of[c]]) for c in chains]
        if diagonal:
            bias = bias_ref[...]
            s = [s[c] + bias for c in chains]
            sm = [_dot_nt(qs[c], km_ref[:, head_cols[head_of[c]]]) for c in chains]
        m_blk = [jnp.max(s[c], axis=-1, keepdims=True) for c in chains]
        if diagonal:
            m_blk = [jnp.maximum(m_blk[c], jnp.max(sm[c], axis=-1, keepdims=True)) for c in chains]
        m_new = [jnp.maximum(carry[3 * c], m_blk[c]) for c in chains]
        alpha = [jnp.exp2(carry[3 * c] - m_new[c]) for c in chains]
        p = [jnp.exp2(s[c] - m_new[c]) for c in chains]
        l_new = [alpha[c] * carry[3 * c + 1] + jnp.sum(p[c], axis=-1, keepdims=True) for c in chains]
        acc = [alpha[c] * carry[3 * c + 2] + _dot(p[c].astype(bf16), vj[head_of[c]]) for c in chains]
        if diagonal:
            pm = [jnp.exp2(sm[c] - m_new[c]) for c in chains]
            l_new = [l_new[c] + jnp.sum(pm[c], axis=-1, keepdims=True) for c in chains]
            acc = [acc[c] + _dot(pm[c].astype(bf16), vm_ref[:, head_cols[head_of[c]]]) for c in chains]
        out = []
        for c in chains:
            out += [m_new[c], l_new[c], acc[c]]
        return tuple(out)

    init = []
    for _ in chains:
        init += [jnp.full((tq, 1), NEG_INF, f32), jnp.zeros((tq, 1), f32), jnp.zeros((tq, V_DIM), f32)]
    carry = lax.fori_loop(0, i, lambda j, c: tile(j, c, False), tuple(init))
    carry = tile(i, carry, True)
    for h in range(hp):
        acc = jnp.concatenate(
            [jnp.concatenate([carry[3 * c + 2], jnp.broadcast_to(carry[3 * c + 1], (tq, V_DIM))], axis=1)
             for c in (2 * h, 2 * h + 1)], axis=0)
        o_ref[0, :, head_cols[h]] = _attn_finish(acc, lp_ref[...], sub_ref[...]).astype(o_ref.dtype)


def _attn_prompt(q, k, v, km, vm, lp, sub, tq, hp):
    b, t, d = q.shape
    w = hp * V_DIM
    qspec = pl.BlockSpec((1, tq, w), lambda bi, h, i: (bi, i, h))
    kspec = pl.BlockSpec((1, t, w), lambda bi, h, i: (bi, 0, h))
    mspec = pl.BlockSpec((N_META, w), lambda bi, h, i: (0, h))
    chunk_of = jnp.arange(tq) // CHUNK
    bias = jnp.where(chunk_of[None, :] <= chunk_of[:, None], 0.0, NEG_INF).astype(f32)
    return pl.pallas_call(
        functools.partial(_attn_prompt_kernel, tq=tq, hp=hp),
        grid=(b, N_HEADS // hp, t // tq),
        in_specs=[qspec, kspec, kspec, mspec, mspec,
                  pl.BlockSpec((tq, tq), lambda bi, h, i: (0, 0), pipeline_mode=pl.Buffered(1)),
                  pl.BlockSpec((4, HEAD_DIM), lambda bi, h, i: (0, 0)),
                  pl.BlockSpec((1, V_DIM), lambda bi, h, i: (0, 0))],
        out_specs=qspec,
        out_shape=jax.ShapeDtypeStruct((b, t, d), bf16),
        compiler_params=_cparams(("parallel", "parallel", "arbitrary")),
        name="attn_prompt",
    )(q, k, v, km, vm, bias, lp, sub)


def _attn_full_kernel(*refs, n_cache):
    if n_cache:
        q_ref, kc_ref, vc_ref, kn_ref, vn_ref, lp_ref, sub_ref, o_ref = refs
    else:
        q_ref, kn_ref, vn_ref, lp_ref, sub_ref, o_ref = refs
    lp = lp_ref[...]
    sub = sub_ref[...]
    for h in range(N_HEADS):
        cols = slice(h * V_DIM, (h + 1) * V_DIM)
        qq = _stack_components(q_ref[0, :, cols])
        keys = [kn_ref[0, :, cols]]
        vals = [vn_ref[0, :, cols]]
        if n_cache:
            for first, count in ((0, N_META), (N_META, n_cache - N_META)):
                rows = pl.ds(first * N_HEADS + h, count, stride=N_HEADS)
                keys.append(kc_ref[0, rows, :].astype(bf16))
                vals.append(vc_ref[0, rows, :].astype(bf16))
        scores = [_dot_nt(qq, kk) for kk in keys]
        m = functools.reduce(jnp.maximum, [jnp.max(s, axis=-1, keepdims=True) for s in scores])
        acc = sum(_dot(jnp.exp2(s - m).astype(bf16), _with_ones(vv)) for s, vv in zip(scores, vals))
        o_ref[0, :, cols] = _attn_finish(acc, lp, sub).astype(o_ref.dtype)


def _attn_full(q, kn, vn, lp, sub, cache=None):
    b, tq, d = q.shape
    spec = pl.BlockSpec((1, tq, d), lambda bi: (bi, 0, 0))
    ins, specs, n_cache = [q], [spec], 0
    if cache is not None:
        n_cache = cache[0].shape[1] // N_HEADS
        cspec = pl.BlockSpec((1, n_cache * N_HEADS, V_DIM), lambda bi: (bi, 0, 0))
        ins += list(cache)
        specs += [cspec, cspec]
    ins += [kn, vn, lp, sub]
    specs += [spec, spec, pl.BlockSpec((4, HEAD_DIM), lambda bi: (0, 0)),
              pl.BlockSpec((1, V_DIM), lambda bi: (0, 0))]
    return pl.pallas_call(
        functools.partial(_attn_full_kernel, n_cache=n_cache),
        grid=(b,),
        in_specs=specs,
        out_specs=spec,
        out_shape=jax.ShapeDtypeStruct((b, tq, d), bf16),
        compiler_params=_cparams(("parallel",)),
        name="attn_full",
    )(*ins)


FRAME_TILE = 512
RWKV_TILE = 256
ATTN_Q_TILE = 512
ATTN_HEADS_PER_STEP = 2
WKV_CHUNK = 64
WKV_SEQS_PER_STEP = 2


def _rope_tables(pos):
    half = ROPE_DIM // 2
    inv = np.power(np.float64(ROPE_THETA), -np.arange(0, ROPE_DIM, 2, dtype=np.float64) / ROPE_DIM)
    ang = pos.astype(np.float64)[:, None] * inv[None, :]
    cos, sin = np.cos(ang).astype(np.float32), np.sin(ang).astype(np.float32)
    n = pos.shape[0]
    one = np.ones((n, HEAD_DIM - ROPE_DIM), np.float32)
    zero = np.zeros((n, HEAD_DIM - ROPE_DIM), np.float32)
    zh = np.zeros((n, half), np.float32)
    c = np.concatenate([cos, cos, one], axis=1)
    sa = np.concatenate([-sin, zh, zero], axis=1)
    sb = np.concatenate([zh, sin, zero], axis=1)
    return tuple(jnp.asarray(np.concatenate([t, t], axis=1), dtype=f32) for t in (c, sa, sb))


def kernel(x_prompt, x_sample, cache_k, cache_v, state_wkv, state_shift, meta_tokens, norm_ffn, norm_mix, ffn_w_gate, ffn_w_up, ffn_w_down, rw_mu, rw_w_r, rw_w_k, rw_w_v, rw_w_o, rw_w0, rw_w1, rw_w2, rw_a0, rw_a1, rw_a2, rw_g1, rw_g2, rw_k_k, rw_k_a, rw_r_k, rw_gn_g, rw_gn_b, kv_norm, w_k, w_v, b_w_q, b_w_o, b_lambda, b_subln, final_norm):
    bp, tp, d = x_prompt.shape
    bd, ts, _ = x_sample.shape
    past = cache_k.shape[1] - N_META
    n_small = N_META + bd * ts
    vec = lambda a: a.reshape(1, -1).astype(f32)
    wb = lambda a: a.astype(bf16)

    head_of_lane = jnp.arange(d) // RW_HEAD
    hs1 = (head_of_lane[:, None] == jnp.arange(LANES)[None, :]).astype(bf16)
    hs2 = hs1.T
    rw = dict(norm_mix=vec(norm_mix[0]), mu=rw_mu[0], w_r=wb(rw_w_r[0]), w_k=wb(rw_w_k[0]), w_v=wb(rw_w_v[0]),
              w_o=wb(rw_w_o[0]), w0=vec(rw_w0[0]), w1=wb(rw_w1[0]), w2=wb(rw_w2[0]), a0=vec(rw_a0[0]),
              a1=wb(rw_a1[0]), a2=wb(rw_a2[0]), g1=wb(rw_g1[0]), g2=wb(rw_g2[0]), k_k=vec(rw_k_k[0]),
              k_a=vec(rw_k_a[0]), r_k=vec(rw_r_k[0]), gn_g=vec(rw_gn_g[0]), gn_b=vec(rw_gn_b[0]),
              hs1=hs1, hs2=hs2)
    wg_all, wu_all, wd_all = wb(ffn_w_gate), wb(ffn_w_up), wb(ffn_w_down)
    ffn_half = lambda x, li, j, **kw: _ffn(x, vec(norm_ffn[li, j]), wg_all, wu_all, wd_all, li, j, **kw)
    wk_b, wv_b, wq_b, wo_b = wb(w_k), wb(w_v), wb(b_w_q[0]), wb(b_w_o[0])
    lp = b_lambda[0].astype(f32)
    sub = vec(b_subln[0])
    fin = vec(final_norm)

    seq = N_META
    xs = jnp.concatenate([meta_tokens.astype(f32), x_sample.reshape(bd * ts, d)], axis=0)
    n_seq_s = n_small // seq
    shift_s0 = jnp.concatenate([jnp.zeros((1, d), f32), state_shift[:, 0]], axis=0)[:, None, :]
    wkv_s0 = jnp.concatenate([jnp.zeros((1,) + state_wkv.shape[2:], f32), state_wkv[:, 0]], axis=0)
    pos_s = np.concatenate([np.arange(N_META), np.tile(N_META + past + np.arange(ts), bd)])
    tabs_s = _rope_tables(pos_s)

    (x1s,) = ffn_half(xs, 0, 0, tm=n_small)
    r, lw, k, v, a_s, b_s, g, hl_s = _rwkv_proj(x1s, shift_s0, rw, tm=seq, tiles_per_seq=1)
    sq = lambda z: z.reshape(n_seq_s, seq, d)
    y, st_s = _wkv(sq(r), sq(lw), sq(k), sq(v), sq(a_s), sq(b_s), wkv_s0, L=seq, nb=1)
    x2s = _rwkv_out(x1s, y.reshape(n_small, d), r, k, v, g, rw, tm=n_small)
    x3s, k_s, v_s, kb_s, vb_s = ffn_half(x2s, 0, 1, tm=n_small, kv=(vec(kv_norm), wk_b, wv_b, tabs_s, 1, None))
    x4s, q_s = ffn_half(x3s, 1, 0, tm=n_small, q=(vec(norm_mix[1]), wq_b, tabs_s, 1))
    o_meta = _attn_full(q_s[None, :N_META], kb_s[None, :N_META], vb_s[None, :N_META], lp, sub)
    bs3 = lambda z: z[N_META:].reshape(bd, ts, d)
    cache_rows = lambda z: z.reshape(bd, (N_META + past) * N_HEADS, V_DIM)
    o_samp = _attn_full(bs3(q_s), bs3(kb_s), bs3(vb_s), lp, sub, cache=(cache_rows(cache_k), cache_rows(cache_v)))
    o_s = jnp.concatenate([o_meta[0], o_samp.reshape(bd * ts, d)], axis=0)
    (ys,) = ffn_half(x4s, 1, 1, tm=n_small, attn=(o_s, wo_b), final_g=fin)

    n_f = bp * tp
    xf = x_prompt.reshape(n_f, d)
    tabs_f = _rope_tables(N_META + np.arange(tp))
    shift_f0 = jnp.broadcast_to(hl_s[0:1], (bp, 1, d))
    wkv_f0 = jnp.broadcast_to(st_s[0:1], (bp,) + st_s.shape[1:])

    (x1,) = ffn_half(xf, 0, 0, tm=FRAME_TILE)
    r, lw, k, v, a_s, b_s, g, hl_f = _rwkv_proj(x1, shift_f0, rw, tm=RWKV_TILE, tiles_per_seq=tp // RWKV_TILE)
    sq = lambda z: z.reshape(bp, tp, d)
    y, st_f = _wkv(sq(r), sq(lw), sq(k), sq(v), sq(a_s), sq(b_s), wkv_f0, L=WKV_CHUNK, nb=WKV_SEQS_PER_STEP)
    x2 = _rwkv_out(x1, y.reshape(n_f, d), r, k, v, g, rw, tm=RWKV_TILE)
    tab_blocks = tp // FRAME_TILE
    x3, k_f, v_f, kb_f, vb_f = ffn_half(
        x2, 0, 1, tm=FRAME_TILE, kv=(vec(kv_norm), wk_b, wv_b, tabs_f, tab_blocks, (bp, N_META + tp, N_META)))
    x4, q_f = ffn_half(x3, 1, 0, tm=FRAME_TILE, q=(vec(norm_mix[1]), wq_b, tabs_f, tab_blocks))
    o_f = _attn_prompt(sq(q_f), sq(kb_f), sq(vb_f), kb_s[:N_META], vb_s[:N_META], lp, sub, tq=ATTN_Q_TILE,
                       hp=ATTN_HEADS_PER_STEP)
    (yf,) = ffn_half(x4, 1, 1, tm=FRAME_TILE, attn=(o_f.reshape(n_f, d), wo_b), final_g=fin)

    y_prompt = yf.reshape(bp, tp, d)
    y_sample = ys[N_META:].reshape(bd, ts, d)
    wkv_p = st_f[:, None]
    wkv_s = st_s[1:][:, None]
    last_tile = tp // RWKV_TILE - 1
    shift_p = hl_f.reshape(bp, tp // RWKV_TILE, d)[:, last_tile][:, None]
    shift_s = hl_s[1:, 0][:, None]
    meta_rows = lambda z: z[:N_META].reshape(N_META * N_HEADS, V_DIM)
    new_k_p = _put_rows(k_f, meta_rows(k_s)).reshape(bp, N_META + tp, N_HEADS, V_DIM)
    new_v_p = _put_rows(v_f, meta_rows(v_s)).reshape(bp, N_META + tp, N_HEADS, V_DIM)
    new_k_s = k_s[N_META:].reshape(bd, ts, N_HEADS, V_DIM)
    new_v_s = v_s[N_META:].reshape(bd, ts, N_HEADS, V_DIM)
    return (y_prompt, y_sample, wkv_p, shift_p, new_k_p, new_v_p, wkv_s, shift_s, new_k_s, new_v_s)
```

```python
import functools
import math

import jax
import jax.numpy as jnp
import numpy as np
from jax import lax
from jax.experimental import pallas as pl
from jax.experimental.pallas import tpu as pltpu

f32 = jnp.float32
bf16 = jnp.bfloat16

D_MODEL = 1024
CHUNK = 64
N_META = 16
RW_HEAD = 64
RW_HEADS = D_MODEL // RW_HEAD
GN_EPS = 64e-5
HEAD_DIM = 64
N_HEADS = D_MODEL // (2 * HEAD_DIM)
V_DIM = 2 * HEAD_DIM
ROPE_DIM = HEAD_DIM // 4
ROPE_THETA = 500000.0
SUBLN_EPS = 1e-5
NEG_INF = -1e30
NORM_EPS = 1e-6
LAM_INIT = 0.8 - 0.6 * math.exp(-0.3 * 1)
LOG2E = math.log2(math.e)
CHUNK_SHIFT = CHUNK.bit_length() - 1

LANES = 128
HEADS_PER_VREG = LANES // RW_HEAD
N_PAIRS = D_MODEL // LANES
VMEM_LIMIT = 56 * 1024 * 1024


def _cparams(sem):
    return pltpu.CompilerParams(dimension_semantics=sem, vmem_limit_bytes=VMEM_LIMIT)


def _dot(a, b):
    return jnp.dot(a, b, preferred_element_type=f32)


def _dot_nt(a, b):
    return lax.dot_general(a, b, (((1,), (1,)), ((), ())), preferred_element_type=f32)


def _dot_tn(a, b):
    return lax.dot_general(a, b, (((0,), (0,)), ((), ())), preferred_element_type=f32)


def _split3(x):
    h1 = x.astype(bf16)
    r1 = x - h1.astype(f32)
    h2 = r1.astype(bf16)
    h3 = (r1 - h2.astype(f32)).astype(bf16)
    return h1, h2, h3


def _sigmoid(x):
    return 1.0 / (1.0 + jnp.exp(-x))


def _rms(x, g, eps):
    return x * lax.rsqrt(jnp.mean(x * x, axis=-1, keepdims=True) + eps) * g


def _headsum_lanes(x):
    first = lax.broadcasted_iota(jnp.int32, (x.shape[0], LANES), 1) < RW_HEAD
    out = []
    for j in range(x.shape[1] // LANES):
        xs = x[:, j * LANES:(j + 1) * LANES]
        both = jnp.sum(xs, axis=-1, keepdims=True)
        head_a = jnp.sum(jnp.where(first, xs, 0.0), axis=-1, keepdims=True)
        out.append(jnp.where(first, head_a, both - head_a))
    return jnp.concatenate(out, axis=1)


def _rope(x, cos, sin_a, sin_b):
    reps = x.shape[1] // LANES
    c = jnp.concatenate([cos] * reps, axis=1)
    sa = jnp.concatenate([sin_a] * reps, axis=1)
    sb = jnp.concatenate([sin_b] * reps, axis=1)
    half = ROPE_DIM // 2
    return x * c + pltpu.roll(x, x.shape[1] - half, 1) * sa + pltpu.roll(x, half, 1) * sb


def _row(i):
    return (i, 0)


def _fixed2(i):
    return (0, 0)


def _rows(tm, w):
    return pl.BlockSpec((tm, w), _row)


def _whole(shape):
    return pl.BlockSpec(shape, _fixed2, pipeline_mode=pl.Buffered(1))


def _ffn_kernel(*refs, attn_in, final_norm, emit_q, emit_kv, cache_layout):
    it = iter(refs)
    x_ref, g_ref, wg_ref, wu_ref, wd_ref = (next(it) for _ in range(5))
    x = x_ref[...]
    if attn_in:
        o_in_ref, wo_ref = next(it), next(it)
        x = x + _dot(o_in_ref[...], wo_ref[...])
    xb = _rms(x, g_ref[...], NORM_EPS).astype(bf16)
    gate = _dot(xb, wg_ref[...])
    up = _dot(xb, wu_ref[...])
    act = (gate * _sigmoid(gate) * up).astype(bf16)
    y = x + 0.5 * _dot(act, wd_ref[...])
    if final_norm:
        y = _rms(y, next(it)[...], NORM_EPS)
    if emit_q:
        nq_ref, wq_ref, cos_ref, sa_ref, sb_ref = (next(it) for _ in range(5))
    if emit_kv:
        nkv_ref, wk_ref, wv_ref, cos_ref, sa_ref, sb_ref = (next(it) for _ in range(6))
    next(it)[...] = y
    if emit_q:
        hb = _rms(y, nq_ref[...], NORM_EPS).astype(bf16)
        q = _rope(_dot(hb, wq_ref[...]), cos_ref[...], sa_ref[...], sb_ref[...])
        next(it)[...] = (q * (HEAD_DIM ** -0.5 * LOG2E)).astype(bf16)
    if emit_kv:
        k_ref, v_ref, kb_ref, vb_ref = (next(it) for _ in range(4))
        hb = _rms(y, nkv_ref[...], NORM_EPS).astype(bf16)
        k = _rope(_dot(hb, wk_ref[...]), cos_ref[...], sa_ref[...], sb_ref[...])
        v = _dot(hb, wv_ref[...])
        kb_ref[...] = k.astype(bf16)
        vb_ref[...] = v.astype(bf16)
        if cache_layout:
            tm = k.shape[0]
            for h in range(N_HEADS):
                k_ref[0, pl.ds(h, tm, stride=N_HEADS), :] = k[:, h * V_DIM:(h + 1) * V_DIM]
                v_ref[0, pl.ds(h, tm, stride=N_HEADS), :] = v[:, h * V_DIM:(h + 1) * V_DIM]
        else:
            k_ref[...] = k
            v_ref[...] = v


def _ffn(x, g, wg, wu, wd, layer, half, tm, attn=None, final_g=None, q=None, kv=None):
    n, d = x.shape
    dff = wg.shape[-1]
    row = _rows(tm, d)

    def pick(rows, cols):
        return pl.BlockSpec((None, None, rows, cols), lambda i: (layer, half, 0, 0), pipeline_mode=pl.Buffered(1))

    ins = [x, g, wg, wu, wd]
    specs = [row, _whole((1, d)), pick(d, dff), pick(d, dff), pick(dff, d)]
    outs = [jax.ShapeDtypeStruct((n, d), f32)]
    out_specs = [row]
    if attn is not None:
        ins += list(attn)
        specs += [row, _whole((d, d))]
    if final_g is not None:
        ins.append(final_g)
        specs.append(_whole((1, d)))
    cache_rows = None
    if q is not None or kv is not None:
        *weights, tabs, tab_blocks = (q if q is not None else kv[:-1])
        tab = pl.BlockSpec((tm, LANES), lambda i: (i % tab_blocks, 0))
        ins += list(weights) + list(tabs)
        specs += [_whole((1, d))] + [_whole((d, d))] * (len(weights) - 1) + [tab] * 3
    if q is not None:
        outs.append(jax.ShapeDtypeStruct((n, d), bf16))
        out_specs.append(row)
    if kv is not None:
        cache_rows = kv[-1]
        if cache_rows is None:
            f32_spec, f32_shape = row, jax.ShapeDtypeStruct((n, d), f32)
        else:
            n_seq, rows_per_seq, first_row = cache_rows
            tiles_per_seq = n // (n_seq * tm)
            f32_spec = pl.BlockSpec(
                (pl.Element(1), pl.Element(tm * N_HEADS), pl.Element(V_DIM)),
                lambda i: (i // tiles_per_seq, (first_row + (i % tiles_per_seq) * tm) * N_HEADS, 0))
            f32_shape = jax.ShapeDtypeStruct((n_seq, rows_per_seq * N_HEADS, V_DIM), f32)
        outs += [f32_shape] * 2 + [jax.ShapeDtypeStruct((n, d), bf16)] * 2
        out_specs += [f32_spec, f32_spec, row, row]
    return pl.pallas_call(
        functools.partial(_ffn_kernel, attn_in=attn is not None, final_norm=final_g is not None,
                          emit_q=q is not None, emit_kv=kv is not None, cache_layout=cache_rows is not None),
        grid=(n // tm,),
        in_specs=specs,
        out_specs=out_specs,
        out_shape=outs,
        compiler_params=_cparams(("parallel",)),
        name="ffn_half",
    )(*ins)


def _rwkv_proj_kernel(x_ref, xprev_ref, shift_ref, nm_ref, mu_ref, wr_ref, wk_ref, wv_ref,
                      w0_ref, w1_ref, w2_ref, a0_ref, a1_ref, a2_ref, g1_ref, g2_ref,
                      kk_ref, ka_ref,
                      r_ref, lw_ref, k_ref, v_ref, as_ref, bs_ref, g_ref, hlast_ref,
                      h_scr, *, tm, tiles_per_seq):
    nm = nm_ref[...]
    h = _rms(x_ref[...], nm, NORM_EPS)
    hlast_ref[0] = h[tm - 1:tm, :]
    first = (pl.program_id(0) % tiles_per_seq) == 0
    h_before = _rms(xprev_ref[7:8, :], nm, NORM_EPS)
    h_scr[8:tm + 8, :] = h
    h_scr[7:8, :] = jnp.where(first, shift_ref[0], h_before)
    xx = h_scr[7:tm + 7, :] - h
    mu = mu_ref[...]

    def mix(m):
        return (h + xx * mu[m:m + 1, :]).astype(bf16)

    r = _dot(mix(0), wr_ref[...])
    k = _dot(mix(2), wk_ref[...])
    v = _dot(mix(3), wv_ref[...])
    ww = w0_ref[...] + _dot(jnp.tanh(_dot(mix(1), w1_ref[...])).astype(bf16), w2_ref[...])
    neg = -ww
    softplus = jnp.maximum(neg, 0.0) + jnp.log(1.0 + jnp.exp(-jnp.abs(neg)))
    lw = -jnp.exp(-softplus - 0.5)
    a = _sigmoid(a0_ref[...] + _dot(_dot(mix(4), a1_ref[...]).astype(bf16), a2_ref[...]))
    g = _dot(_sigmoid(_dot(mix(5), g1_ref[...])).astype(bf16), g2_ref[...])
    kk = k * kk_ref[...]
    norm = jnp.sqrt(_headsum_lanes(kk * kk))
    kk = kk / jnp.maximum(norm, 1e-12)
    r_ref[...] = r
    lw_ref[...] = lw
    k_ref[...] = k * (1.0 + (a - 1.0) * ka_ref[...])
    v_ref[...] = v
    as_ref[...] = -kk
    bs_ref[...] = kk * a
    g_ref[...] = g


def _rwkv_proj(x, shift0, p, tm, tiles_per_seq):
    n, d = x.shape
    nt = n // tm
    row = _rows(tm, d)
    lw_, la_, lg_ = p["w1"].shape[1], p["a1"].shape[1], p["g1"].shape[1]
    specs = [
        row,
        pl.BlockSpec((8, d), lambda i: (jnp.maximum(i * (tm // 8) - 1, 0), 0)),
        pl.BlockSpec((1, 1, d), lambda i: (i // tiles_per_seq, 0, 0)),
        _whole((1, d)), _whole((6, d)),
        _whole((d, d)), _whole((d, d)), _whole((d, d)),
        _whole((1, d)), _whole((d, lw_)), _whole((lw_, d)),
        _whole((1, d)), _whole((d, la_)), _whole((la_, d)),
        _whole((d, lg_)), _whole((lg_, d)),
        _whole((1, d)), _whole((1, d)),
    ]
    outs = [jax.ShapeDtypeStruct((n, d), f32)] * 7 + [jax.ShapeDtypeStruct((nt, 1, d), f32)]
    out_specs = [row] * 7 + [pl.BlockSpec((1, 1, d), lambda i: (i, 0, 0))]
    return pl.pallas_call(
        functools.partial(_rwkv_proj_kernel, tm=tm, tiles_per_seq=tiles_per_seq),
        grid=(nt,),
        in_specs=specs,
        out_specs=out_specs,
        out_shape=outs,
        scratch_shapes=[pltpu.VMEM((tm + 8, d), f32)],
        compiler_params=_cparams(("parallel",)),
        name="rwkv_proj",
    )(x, x, shift0, p["norm_mix"], p["mu"], p["w_r"], p["w_k"], p["w_v"],
      p["w0"], p["w1"], p["w2"], p["a0"], p["a1"], p["a2"], p["g1"], p["g2"],
      p["k_k"], p["k_a"])


def _wkv_kernel(r_ref, lw_ref, k_ref, v_ref, a_ref, b_ref, g_ref, rk_ref, gg_ref, gb_ref, s0_ref,
                z_ref, sT_ref, s_scr, *, L, nb):
    c = pl.program_id(1)
    seqs = range(nb)
    units = range(nb * N_PAIRS)

    @pl.when(c == 0)
    def _():
        zero = jnp.zeros((RW_HEAD, RW_HEAD), f32)
        for u in units:
            b, p = divmod(u, N_PAIRS)
            top = jnp.concatenate([s0_ref[b, HEADS_PER_VREG * p], zero], axis=1)
            bottom = jnp.concatenate([zero, s0_ref[b, HEADS_PER_VREG * p + 1]], axis=1)
            s_scr[u] = jnp.concatenate([top, bottom], axis=0).T

    L2 = 2 * L
    ri = lax.broadcasted_iota(jnp.int32, (L, L), 0)
    ci = lax.broadcasted_iota(jnp.int32, (L, L), 1)
    cum = jnp.where(ri >= ci, 1.0, 0.0).astype(bf16)
    rs = lax.broadcasted_iota(jnp.int32, (L2, LANES), 0)
    ls = lax.broadcasted_iota(jnp.int32, (L2, LANES), 1)
    own = (rs < L) == (ls < RW_HEAD)
    rg = lax.broadcasted_iota(jnp.int32, (L2, L2), 0) & (L - 1)
    cg = lax.broadcasted_iota(jnp.int32, (L2, L2), 1) & (L - 1)
    strict = rg > cg
    incl = rg >= cg

    lw = [lw_ref[b] for b in seqs]
    parts = [_split3(lw[b]) for b in seqs]
    cs = [_dot(cum, parts[b][0]) + _dot(cum, parts[b][1]) + _dot(cum, parts[b][2]) for b in seqs]
    w_in = [jnp.exp(cs[b]) for b in seqs]
    w_inv = [jnp.exp(-cs[b]) for b in seqs]
    pairs = units

    def stack(xs):
        out = []
        for u in units:
            b, p = divmod(u, N_PAIRS)
            xp = xs[b][:, p * LANES:(p + 1) * LANES]
            out.append(jnp.where(own, jnp.concatenate([xp, xp], axis=0), 0.0).astype(bf16))
        return out

    a_st = stack([a_ref[b] * jnp.exp(cs[b] - lw[b]) for b in seqs])
    r_st = stack([r_ref[b] * w_in[b] for b in seqs])
    b_st = stack([b_ref[b] * w_inv[b] for b in seqs])
    k_st = stack([k_ref[b] * w_inv[b] for b in seqs])
    v_st = stack([v_ref[b] for b in seqs])
    gram = [_dot_nt(jnp.concatenate([a_st[p], r_st[p]], axis=0), jnp.concatenate([b_st[p], k_st[p]], axis=0))
            for p in pairs]
    m = [jnp.where(strict, gram[p][:L2, :L2], 0.0) for p in pairs]
    a_ak = [jnp.where(strict, gram[p][:L2, L2:], 0.0).astype(bf16) for p in pairs]
    a_rb = [jnp.where(incl, gram[p][L2:, :L2], 0.0).astype(bf16) for p in pairs]
    a_rk = [jnp.where(incl, gram[p][L2:, L2:], 0.0).astype(bf16) for p in pairs]
    s = [s_scr[p] for p in pairs]
    sb = [s[p].astype(bf16) for p in pairs]

    rhs = [_dot(jnp.concatenate([a_st[p], a_ak[p]], axis=1), jnp.concatenate([sb[p], v_st[p]], axis=0))
           for p in pairs]
    t = m
    mb = [m[p].astype(bf16) for p in pairs]
    m = [_dot(mb[p], mb[p]) for p in pairs]
    n_sq = L.bit_length() - 2
    for k in range(1, n_sq + 1):
        mb = [m[p].astype(bf16) for p in pairs]
        if k < n_sq:
            both = [_dot(mb[p], jnp.concatenate([t[p].astype(bf16), mb[p]], axis=1)) for p in pairs]
            t = [t[p] + m[p] + both[p][:, :L2] for p in pairs]
            m = [both[p][:, L2:] for p in pairs]
        else:
            t = [t[p] + m[p] + _dot(mb[p], t[p].astype(bf16)) for p in pairs]
    ub = [(rhs[p] + _dot(t[p].astype(bf16), rhs[p].astype(bf16))).astype(bf16) for p in pairs]
    y_st = [_dot(jnp.concatenate([r_st[p], a_rb[p], a_rk[p]], axis=1),
                 jnp.concatenate([sb[p], ub[p], v_st[p]], axis=0)) for p in pairs]
    for b in seqs:
        y = jnp.concatenate([y_st[u][:L] + y_st[u][L:] for u in units[b * N_PAIRS:(b + 1) * N_PAIRS]], axis=1)
        d = y - _headsum_lanes(y) * (1.0 / RW_HEAD)
        var = _headsum_lanes(d * d) * (1.0 / RW_HEAD)
        yn = d * lax.rsqrt(var + GN_EPS) * gg_ref[...] + gb_ref[...]
        bonus = _headsum_lanes(r_ref[b] * k_ref[b] * rk_ref[...]) * v_ref[b]
        z_ref[b] = ((yn + bonus) * g_ref[b]).astype(z_ref.dtype)
    eye = (lax.broadcasted_iota(jnp.int32, (LANES, LANES), 0)
           == lax.broadcasted_iota(jnp.int32, (LANES, LANES), 1)).astype(f32)
    for u in units:
        b, p = divmod(u, N_PAIRS)
        w_rows = jnp.sum(eye * w_in[b][L - 1:L, p * LANES:(p + 1) * LANES], axis=1, keepdims=True)
        s_scr[u] = (s[u] + _dot_tn(jnp.concatenate([b_st[u], k_st[u]], axis=0),
                                   jnp.concatenate([ub[u], v_st[u]], axis=0))) * w_rows

    @pl.when(c == pl.num_programs(1) - 1)
    def _():
        for u in units:
            b, p = divmod(u, N_PAIRS)
            s_vk = s_scr[u].T
            sT_ref[b, HEADS_PER_VREG * p] = s_vk[:RW_HEAD, :RW_HEAD]
            sT_ref[b, HEADS_PER_VREG * p + 1] = s_vk[RW_HEAD:, RW_HEAD:]


def _wkv(r, lw, k, v, a, b, g, p, s0, L, nb):
    nseq, t, d = r.shape
    blk = pl.BlockSpec((nb, L, d), lambda s, c: (s, c, 0))
    st = pl.BlockSpec((nb, RW_HEADS, RW_HEAD, RW_HEAD), lambda s, c: (s, 0, 0, 0))
    vec = pl.BlockSpec((1, d), lambda s, c: (0, 0))
    return pl.pallas_call(
        functools.partial(_wkv_kernel, L=L, nb=nb),
        grid=(nseq // nb, t // L),
        in_specs=[blk] * 7 + [vec] * 3 + [st],
        out_specs=[blk, st],
        out_shape=[jax.ShapeDtypeStruct((nseq, t, d), bf16),
                   jax.ShapeDtypeStruct((nseq, RW_HEADS, RW_HEAD, RW_HEAD), f32)],
        scratch_shapes=[pltpu.VMEM((nb * N_PAIRS, LANES, LANES), f32)],
        compiler_params=_cparams(("parallel", "arbitrary")),
        name="wkv_scan",
    )(r, lw, k, v, a, b, g, p["r_k"], p["gn_g"], p["gn_b"], s0)


def _put_rows_kernel(big_ref, rows_ref, o_ref):
    del big_ref
    o_ref[0] = rows_ref[...]


def _put_rows(big, rows):
    n_seq = big.shape[0]
    r = rows.shape[0]
    return pl.pallas_call(
        _put_rows_kernel,
        grid=(n_seq,),
        in_specs=[pl.BlockSpec(memory_space=pl.ANY), pl.BlockSpec((r, LANES), lambda s: (0, 0))],
        out_specs=pl.BlockSpec((1, r, LANES), lambda s: (s, 0, 0)),
        out_shape=jax.ShapeDtypeStruct(big.shape, big.dtype),
        input_output_aliases={0: 0},
        compiler_params=_cparams(("parallel",)),
        name="put_meta_rows",
    )(big, rows)


def _lam(lp):
    e1 = jnp.exp(jnp.sum(lp[0:1, :] * lp[1:2, :], axis=-1, keepdims=True))
    e2 = jnp.exp(jnp.sum(lp[2:3, :] * lp[3:4, :], axis=-1, keepdims=True))
    return e1 - e2 + LAM_INIT


def _split_heads(q):
    lane = lax.broadcasted_iota(jnp.int32, q.shape, 1)
    zero = jnp.zeros_like(q)
    return jnp.where(lane < HEAD_DIM, q, zero), jnp.where(lane >= HEAD_DIM, q, zero)


def _stack_components(q):
    q1, q2 = _split_heads(q)
    return jnp.concatenate([q1, q2], axis=0)


def _with_ones(v):
    return jnp.concatenate([v, jnp.ones(v.shape, v.dtype)], axis=1)


def _attn_finish(acc, lp, sub):
    n = acc.shape[0] // 2
    o = acc[:n, :V_DIM] / acc[:n, V_DIM:] - _lam(lp) * (acc[n:, :V_DIM] / acc[n:, V_DIM:])
    o = o * lax.rsqrt(jnp.mean(o * o, axis=-1, keepdims=True) + SUBLN_EPS) * sub
    return o * (1.0 - LAM_INIT)


def _attn_prompt_kernel(q_ref, k_ref, v_ref, km_ref, vm_ref, bias_ref, lp_ref, sub_ref, o_ref, *, tq, hp):
    i = pl.program_id(2)
    head_cols = [slice(h * V_DIM, (h + 1) * V_DIM) for h in range(hp)]
    qs, head_of = [], []
    for h in range(hp):
        qs += list(_split_heads(q_ref[0, :, head_cols[h]]))
        head_of += [h, h]
    chains = range(len(qs))

    def scores(j):
        start = pl.multiple_of(j * tq, tq)
        kj = [k_ref[0, pl.ds(start, tq), head_cols[h]] for h in range(hp)]
        return [_dot_nt(qs[c], kj[head_of[c]]) for c in chains]

    def tile(j, s, carry, diagonal):
        start = pl.multiple_of(j * tq, tq)
        vj = [v_ref[0, pl.ds(start, tq), head_cols[h]] for h in range(hp)]
        if diagonal:
            bias = bias_ref[...]
            s = [s[c] + bias for c in chains]
            sm = [_dot_nt(qs[c], km_ref[:, head_cols[head_of[c]]]) for c in chains]
        m_blk = [jnp.max(s[c], axis=-1, keepdims=True) for c in chains]
        if diagonal:
            m_blk = [jnp.maximum(m_blk[c], jnp.max(sm[c], axis=-1, keepdims=True)) for c in chains]
        m_new = [jnp.maximum(carry[3 * c], m_blk[c]) for c in chains]
        alpha = [jnp.exp2(carry[3 * c] - m_new[c]) for c in chains]
        p = [jnp.exp2(s[c] - m_new[c]) for c in chains]
        l_new = [alpha[c] * carry[3 * c + 1] + jnp.sum(p[c], axis=-1, keepdims=True) for c in chains]
        acc = [alpha[c] * carry[3 * c + 2] + _dot(p[c].astype(bf16), vj[head_of[c]]) for c in chains]
        if diagonal:
            pm = [jnp.exp2(sm[c] - m_new[c]) for c in chains]
            l_new = [l_new[c] + jnp.sum(pm[c], axis=-1, keepdims=True) for c in chains]
            acc = [acc[c] + _dot(pm[c].astype(bf16), vm_ref[:, head_cols[head_of[c]]]) for c in chains]
        out = []
        for c in chains:
            out += [m_new[c], l_new[c], acc[c]]
        return tuple(out)

    init = []
    for _ in chains:
        init += [jnp.full((tq, 1), NEG_INF, f32), jnp.zeros((tq, 1), f32), jnp.zeros((tq, V_DIM), f32)]
    carry = lax.fori_loop(0, i, lambda j, c: tile(j, scores(j), c, False), tuple(init))
    carry = tile(i, scores(i), carry, True)
    for h in range(hp):
        acc = jnp.concatenate(
            [jnp.concatenate([carry[3 * c + 2], jnp.broadcast_to(carry[3 * c + 1], (tq, V_DIM))], axis=1)
             for c in (2 * h, 2 * h + 1)], axis=0)
        o_ref[0, :, head_cols[h]] = _attn_finish(acc, lp_ref[...], sub_ref[...]).astype(o_ref.dtype)


def _attn_prompt(q, k, v, km, vm, lp, sub, tq, hp):
    b, t, d = q.shape
    w = hp * V_DIM
    qspec = pl.BlockSpec((1, tq, w), lambda bi, h, i: (bi, i, h))
    kspec = pl.BlockSpec((1, t, w), lambda bi, h, i: (bi, 0, h))
    mspec = pl.BlockSpec((N_META, w), lambda bi, h, i: (0, h))
    chunk_of = jnp.arange(tq) // CHUNK
    bias = jnp.where(chunk_of[None, :] <= chunk_of[:, None], 0.0, NEG_INF).astype(f32)
    return pl.pallas_call(
        functools.partial(_attn_prompt_kernel, tq=tq, hp=hp),
        grid=(b, N_HEADS // hp, t // tq),
        in_specs=[qspec, kspec, kspec, mspec, mspec,
                  pl.BlockSpec((tq, tq), lambda bi, h, i: (0, 0), pipeline_mode=pl.Buffered(1)),
                  pl.BlockSpec((4, HEAD_DIM), lambda bi, h, i: (0, 0)),
                  pl.BlockSpec((1, V_DIM), lambda bi, h, i: (0, 0))],
        out_specs=qspec,
        out_shape=jax.ShapeDtypeStruct((b, t, d), bf16),
        compiler_params=_cparams(("parallel", "parallel", "arbitrary")),
        name="attn_prompt",
    )(q, k, v, km, vm, bias, lp, sub)


def _attn_full_kernel(*refs, n_cache):
    if n_cache:
        q_ref, kc_ref, vc_ref, kn_ref, vn_ref, lp_ref, sub_ref, o_ref = refs
    else:
        q_ref, kn_ref, vn_ref, lp_ref, sub_ref, o_ref = refs
    lp = lp_ref[...]
    sub = sub_ref[...]
    for h in range(N_HEADS):
        cols = slice(h * V_DIM, (h + 1) * V_DIM)
        qq = _stack_components(q_ref[0, :, cols])
        keys = [kn_ref[0, :, cols]]
        vals = [vn_ref[0, :, cols]]
        if n_cache:
            for first, count in ((0, N_META), (N_META, n_cache - N_META)):
                rows = pl.ds(first * N_HEADS + h, count, stride=N_HEADS)
                keys.append(kc_ref[0, rows, :].astype(bf16))
                vals.append(vc_ref[0, rows, :].astype(bf16))
        scores = [_dot_nt(qq, kk) for kk in keys]
        m = functools.reduce(jnp.maximum, [jnp.max(s, axis=-1, keepdims=True) for s in scores])
        acc = sum(_dot(jnp.exp2(s - m).astype(bf16), _with_ones(vv)) for s, vv in zip(scores, vals))
        o_ref[0, :, cols] = _attn_finish(acc, lp, sub).astype(o_ref.dtype)


def _attn_full(q, kn, vn, lp, sub, cache=None):
    b, tq, d = q.shape
    spec = pl.BlockSpec((1, tq, d), lambda bi: (bi, 0, 0))
    ins, specs, n_cache = [q], [spec], 0
    if cache is not None:
        n_cache = cache[0].shape[1] // N_HEADS
        cspec = pl.BlockSpec((1, n_cache * N_HEADS, V_DIM), lambda bi: (bi, 0, 0))
        ins += list(cache)
        specs += [cspec, cspec]
    ins += [kn, vn, lp, sub]
    specs += [spec, spec, pl.BlockSpec((4, HEAD_DIM), lambda bi: (0, 0)),
              pl.BlockSpec((1, V_DIM), lambda bi: (0, 0))]
    return pl.pallas_call(
        functools.partial(_attn_full_kernel, n_cache=n_cache),
        grid=(b,),
        in_specs=specs,
        out_specs=spec,
        out_shape=jax.ShapeDtypeStruct((b, tq, d), bf16),
        compiler_params=_cparams(("parallel",)),
        name="attn_full",
    )(*ins)


FRAME_TILE = 512
RWKV_TILE = 256
ATTN_Q_TILE = 512
ATTN_HEADS_PER_STEP = 2
WKV_CHUNK = 64
WKV_SMALL_SEQS_PER_STEP = 3
WKV_SEQS_PER_STEP = 2


def _rope_tables(pos):
    half = ROPE_DIM // 2
    inv = np.power(np.float64(ROPE_THETA), -np.arange(0, ROPE_DIM, 2, dtype=np.float64) / ROPE_DIM)
    ang = pos.astype(np.float64)[:, None] * inv[None, :]
    cos, sin = np.cos(ang).astype(np.float32), np.sin(ang).astype(np.float32)
    n = pos.shape[0]
    one = np.ones((n, HEAD_DIM - ROPE_DIM), np.float32)
    zero = np.zeros((n, HEAD_DIM - ROPE_DIM), np.float32)
    zh = np.zeros((n, half), np.float32)
    c = np.concatenate([cos, cos, one], axis=1)
    sa = np.concatenate([-sin, zh, zero], axis=1)
    sb = np.concatenate([zh, sin, zero], axis=1)
    return tuple(jnp.asarray(np.concatenate([t, t], axis=1), dtype=f32) for t in (c, sa, sb))


def kernel(x_prompt, x_sample, cache_k, cache_v, state_wkv, state_shift, meta_tokens, norm_ffn, norm_mix, ffn_w_gate, ffn_w_up, ffn_w_down, rw_mu, rw_w_r, rw_w_k, rw_w_v, rw_w_o, rw_w0, rw_w1, rw_w2, rw_a0, rw_a1, rw_a2, rw_g1, rw_g2, rw_k_k, rw_k_a, rw_r_k, rw_gn_g, rw_gn_b, kv_norm, w_k, w_v, b_w_q, b_w_o, b_lambda, b_subln, final_norm):
    bp, tp, d = x_prompt.shape
    bd, ts, _ = x_sample.shape
    past = cache_k.shape[1] - N_META
    n_small = N_META + bd * ts
    vec = lambda a: a.reshape(1, -1).astype(f32)
    wb = lambda a: a.astype(bf16)

    rw = dict(norm_mix=vec(norm_mix[0]), mu=rw_mu[0], w_r=wb(rw_w_r[0]), w_k=wb(rw_w_k[0]), w_v=wb(rw_w_v[0]),
              w_o=wb(rw_w_o[0]), w0=vec(rw_w0[0]), w1=wb(rw_w1[0]), w2=wb(rw_w2[0]), a0=vec(rw_a0[0]),
              a1=wb(rw_a1[0]), a2=wb(rw_a2[0]), g1=wb(rw_g1[0]), g2=wb(rw_g2[0]), k_k=vec(rw_k_k[0]),
              k_a=vec(rw_k_a[0]), r_k=vec(rw_r_k[0]), gn_g=vec(rw_gn_g[0]), gn_b=vec(rw_gn_b[0]))
    wg_all, wu_all, wd_all = wb(ffn_w_gate), wb(ffn_w_up), wb(ffn_w_down)
    ffn_half = lambda x, li, j, **kw: _ffn(x, vec(norm_ffn[li, j]), wg_all, wu_all, wd_all, li, j, **kw)
    wk_b, wv_b, wq_b, wo_b = wb(w_k), wb(w_v), wb(b_w_q[0]), wb(b_w_o[0])
    lp = b_lambda[0].astype(f32)
    sub = vec(b_subln[0])
    fin = vec(final_norm)

    seq = N_META
    xs = jnp.concatenate([meta_tokens.astype(f32), x_sample.reshape(bd * ts, d)], axis=0)
    n_seq_s = n_small // seq
    shift_s0 = jnp.concatenate([jnp.zeros((1, d), f32), state_shift[:, 0]], axis=0)[:, None, :]
    wkv_s0 = jnp.concatenate([jnp.zeros((1,) + state_wkv.shape[2:], f32), state_wkv[:, 0]], axis=0)
    pos_s = np.concatenate([np.arange(N_META), np.tile(N_META + past + np.arange(ts), bd)])
    tabs_s = _rope_tables(pos_s)

    (x1s,) = ffn_half(xs, 0, 0, tm=n_small)
    r, lw, k, v, a_s, b_s, g, hl_s = _rwkv_proj(x1s, shift_s0, rw, tm=seq, tiles_per_seq=1)
    sq = lambda z: z.reshape(n_seq_s, seq, d)
    z, st_s = _wkv(sq(r), sq(lw), sq(k), sq(v), sq(a_s), sq(b_s), sq(g), rw, wkv_s0, L=seq, nb=WKV_SMALL_SEQS_PER_STEP)
    x3s, k_s, v_s, kb_s, vb_s = ffn_half(x1s, 0, 1, tm=n_small, attn=(z.reshape(n_small, d), rw["w_o"]),
                                         kv=(vec(kv_norm), wk_b, wv_b, tabs_s, 1, None))
    x4s, q_s = ffn_half(x3s, 1, 0, tm=n_small, q=(vec(norm_mix[1]), wq_b, tabs_s, 1))
    o_meta = _attn_full(q_s[None, :N_META], kb_s[None, :N_META], vb_s[None, :N_META], lp, sub)
    bs3 = lambda z: z[N_META:].reshape(bd, ts, d)
    cache_rows = lambda z: z.reshape(bd, (N_META + past) * N_HEADS, V_DIM)
    o_samp = _attn_full(bs3(q_s), bs3(kb_s), bs3(vb_s), lp, sub, cache=(cache_rows(cache_k), cache_rows(cache_v)))
    o_s = jnp.concatenate([o_meta[0], o_samp.reshape(bd * ts, d)], axis=0)
    (ys,) = ffn_half(x4s, 1, 1, tm=n_small, attn=(o_s, wo_b), final_g=fin)

    n_f = bp * tp
    xf = x_prompt.reshape(n_f, d)
    tabs_f = _rope_tables(N_META + np.arange(tp))
    shift_f0 = jnp.broadcast_to(hl_s[0:1], (bp, 1, d))
    wkv_f0 = jnp.broadcast_to(st_s[0:1], (bp,) + st_s.shape[1:])

    (x1,) = ffn_half(xf, 0, 0, tm=FRAME_TILE)
    r, lw, k, v, a_s, b_s, g, hl_f = _rwkv_proj(x1, shift_f0, rw, tm=RWKV_TILE, tiles_per_seq=tp // RWKV_TILE)
    sq = lambda z: z.reshape(bp, tp, d)
    z, st_f = _wkv(sq(r), sq(lw), sq(k), sq(v), sq(a_s), sq(b_s), sq(g), rw, wkv_f0, L=WKV_CHUNK, nb=WKV_SEQS_PER_STEP)
    tab_blocks = tp // FRAME_TILE
    x3, k_f, v_f, kb_f, vb_f = ffn_half(
        x1, 0, 1, tm=FRAME_TILE, attn=(z.reshape(n_f, d), rw["w_o"]),
        kv=(vec(kv_norm), wk_b, wv_b, tabs_f, tab_blocks, (bp, N_META + tp, N_META)))
    x4, q_f = ffn_half(x3, 1, 0, tm=FRAME_TILE, q=(vec(norm_mix[1]), wq_b, tabs_f, tab_blocks))
    o_f = _attn_prompt(sq(q_f), sq(kb_f), sq(vb_f), kb_s[:N_META], vb_s[:N_META], lp, sub, tq=ATTN_Q_TILE,
                       hp=ATTN_HEADS_PER_STEP)
    (yf,) = ffn_half(x4, 1, 1, tm=FRAME_TILE, attn=(o_f.reshape(n_f, d), wo_b), final_g=fin)

    y_prompt = yf.reshape(bp, tp, d)
    y_sample = ys[N_META:].reshape(bd, ts, d)
    wkv_p = st_f[:, None]
    wkv_s = st_s[1:][:, None]
    last_tile = tp // RWKV_TILE - 1
    shift_p = hl_f.reshape(bp, tp // RWKV_TILE, d)[:, last_tile][:, None]
    shift_s = hl_s[1:, 0][:, None]
    meta_rows = lambda z: z[:N_META].reshape(N_META * N_HEADS, V_DIM)
    new_k_p = _put_rows(k_f, meta_rows(k_s)).reshape(bp, N_META + tp, N_HEADS, V_DIM)
    new_v_p = _put_rows(v_f, meta_rows(v_s)).reshape(bp, N_META + tp, N_HEADS, V_DIM)
    new_k_s = k_s[N_META:].reshape(bd, ts, N_HEADS, V_DIM)
    new_v_s = v_s[N_META:].reshape(bd, ts, N_HEADS, V_DIM)
    return (y_prompt, y_sample, wkv_p, shift_p, new_k_p, new_v_p, wkv_s, shift_s, new_k_s, new_v_s)
```

```python
import functools
import math

import jax
import jax.numpy as jnp
import numpy as np
from jax import lax
from jax.experimental import pallas as pl
from jax.experimental.pallas import tpu as pltpu

f32 = jnp.float32
bf16 = jnp.bfloat16

D_MODEL = 1024
CHUNK = 64
N_META = 16
RW_HEAD = 64
RW_HEADS = D_MODEL // RW_HEAD
GN_EPS = 64e-5
HEAD_DIM = 64
N_HEADS = D_MODEL // (2 * HEAD_DIM)
V_DIM = 2 * HEAD_DIM
ROPE_DIM = HEAD_DIM // 4
ROPE_THETA = 500000.0
SUBLN_EPS = 1e-5
NEG_INF = -1e30
NORM_EPS = 1e-6
LAM_INIT = 0.8 - 0.6 * math.exp(-0.3 * 1)
LOG2E = math.log2(math.e)
CHUNK_SHIFT = CHUNK.bit_length() - 1

LANES = 128
HEADS_PER_VREG = LANES // RW_HEAD
N_PAIRS = D_MODEL // LANES
VMEM_LIMIT = 56 * 1024 * 1024


def _cparams(sem):
    return pltpu.CompilerParams(dimension_semantics=sem, vmem_limit_bytes=VMEM_LIMIT)


def _dot(a, b):
    return jnp.dot(a, b, preferred_element_type=f32)


def _dot_nt(a, b):
    return lax.dot_general(a, b, (((1,), (1,)), ((), ())), preferred_element_type=f32)


def _dot_tn(a, b):
    return lax.dot_general(a, b, (((0,), (0,)), ((), ())), preferred_element_type=f32)


def _split3(x):
    h1 = x.astype(bf16)
    r1 = x - h1.astype(f32)
    h2 = r1.astype(bf16)
    h3 = (r1 - h2.astype(f32)).astype(bf16)
    return h1, h2, h3


def _sigmoid(x):
    return 1.0 / (1.0 + jnp.exp(-x))


def _rms(x, g, eps):
    return x * lax.rsqrt(jnp.mean(x * x, axis=-1, keepdims=True) + eps) * g


def _headsum_lanes(x):
    first = lax.broadcasted_iota(jnp.int32, (x.shape[0], LANES), 1) < RW_HEAD
    out = []
    for j in range(x.shape[1] // LANES):
        xs = x[:, j * LANES:(j + 1) * LANES]
        both = jnp.sum(xs, axis=-1, keepdims=True)
        head_a = jnp.sum(jnp.where(first, xs, 0.0), axis=-1, keepdims=True)
        out.append(jnp.where(first, head_a, both - head_a))
    return jnp.concatenate(out, axis=1)


def _rope(x, cos, sin_a, sin_b):
    reps = x.shape[1] // LANES
    c = jnp.concatenate([cos] * reps, axis=1)
    sa = jnp.concatenate([sin_a] * reps, axis=1)
    sb = jnp.concatenate([sin_b] * reps, axis=1)
    half = ROPE_DIM // 2
    return x * c + pltpu.roll(x, x.shape[1] - half, 1) * sa + pltpu.roll(x, half, 1) * sb


def _row(i):
    return (i, 0)


def _fixed2(i):
    return (0, 0)


def _rows(tm, w):
    return pl.BlockSpec((tm, w), _row)


def _whole(shape):
    return pl.BlockSpec(shape, _fixed2, pipeline_mode=pl.Buffered(1))


def _ffn_kernel(*refs, attn_in, final_norm, emit_q, emit_kv, cache_layout):
    it = iter(refs)
    x_ref, g_ref, wg_ref, wu_ref, wd_ref = (next(it) for _ in range(5))
    x = x_ref[...]
    if attn_in:
        o_in_ref, wo_ref = next(it), next(it)
        x = x + _dot(o_in_ref[...], wo_ref[...])
    xb = _rms(x, g_ref[...], NORM_EPS).astype(bf16)
    gate = _dot(xb, wg_ref[...])
    up = _dot(xb, wu_ref[...])
    act = (gate * _sigmoid(gate) * up).astype(bf16)
    y = x + 0.5 * _dot(act, wd_ref[...])
    if final_norm:
        y = _rms(y, next(it)[...], NORM_EPS)
    if emit_q:
        nq_ref, wq_ref, cos_ref, sa_ref, sb_ref = (next(it) for _ in range(5))
    if emit_kv:
        nkv_ref, wk_ref, wv_ref, cos_ref, sa_ref, sb_ref = (next(it) for _ in range(6))
    next(it)[...] = y
    if emit_q:
        hb = _rms(y, nq_ref[...], NORM_EPS).astype(bf16)
        q = _rope(_dot(hb, wq_ref[...]), cos_ref[...], sa_ref[...], sb_ref[...])
        next(it)[...] = (q * (HEAD_DIM ** -0.5 * LOG2E)).astype(bf16)
    if emit_kv:
        k_ref, v_ref, kb_ref, vb_ref = (next(it) for _ in range(4))
        hb = _rms(y, nkv_ref[...], NORM_EPS).astype(bf16)
        k = _rope(_dot(hb, wk_ref[...]), cos_ref[...], sa_ref[...], sb_ref[...])
        v = _dot(hb, wv_ref[...])
        kb_ref[...] = k.astype(bf16)
        vb_ref[...] = v.astype(bf16)
        if cache_layout:
            tm = k.shape[0]
            for h in range(N_HEADS):
                k_ref[0, pl.ds(h, tm, stride=N_HEADS), :] = k[:, h * V_DIM:(h + 1) * V_DIM]
                v_ref[0, pl.ds(h, tm, stride=N_HEADS), :] = v[:, h * V_DIM:(h + 1) * V_DIM]
        else:
            k_ref[...] = k
            v_ref[...] = v


def _ffn(x, g, wg, wu, wd, layer, half, tm, attn=None, final_g=None, q=None, kv=None):
    n, d = x.shape
    dff = wg.shape[-1]
    row = _rows(tm, d)

    def pick(rows, cols):
        return pl.BlockSpec((None, None, rows, cols), lambda i: (layer, half, 0, 0), pipeline_mode=pl.Buffered(1))

    ins = [x, g, wg, wu, wd]
    specs = [row, _whole((1, d)), pick(d, dff), pick(d, dff), pick(dff, d)]
    outs = [jax.ShapeDtypeStruct((n, d), f32)]
    out_specs = [row]
    if attn is not None:
        ins += list(attn)
        specs += [row, _whole((d, d))]
    if final_g is not None:
        ins.append(final_g)
        specs.append(_whole((1, d)))
    cache_rows = None
    if q is not None or kv is not None:
        *weights, tabs, tab_blocks = (q if q is not None else kv[:-1])
        tab = pl.BlockSpec((tm, LANES), lambda i: (i % tab_blocks, 0))
        ins += list(weights) + list(tabs)
        specs += [_whole((1, d))] + [_whole((d, d))] * (len(weights) - 1) + [tab] * 3
    if q is not None:
        outs.append(jax.ShapeDtypeStruct((n, d), bf16))
        out_specs.append(row)
    if kv is not None:
        cache_rows = kv[-1]
        if cache_rows is None:
            f32_spec, f32_shape = row, jax.ShapeDtypeStruct((n, d), f32)
        else:
            n_seq, rows_per_seq, first_row = cache_rows
            tiles_per_seq = n // (n_seq * tm)
            f32_spec = pl.BlockSpec(
                (pl.Element(1), pl.Element(tm * N_HEADS), pl.Element(V_DIM)),
                lambda i: (i // tiles_per_seq, (first_row + (i % tiles_per_seq) * tm) * N_HEADS, 0))
            f32_shape = jax.ShapeDtypeStruct((n_seq, rows_per_seq * N_HEADS, V_DIM), f32)
        outs += [f32_shape] * 2 + [jax.ShapeDtypeStruct((n, d), bf16)] * 2
        out_specs += [f32_spec, f32_spec, row, row]
    return pl.pallas_call(
        functools.partial(_ffn_kernel, attn_in=attn is not None, final_norm=final_g is not None,
                          emit_q=q is not None, emit_kv=kv is not None, cache_layout=cache_rows is not None),
        grid=(n // tm,),
        in_specs=specs,
        out_specs=out_specs,
        out_shape=outs,
        compiler_params=_cparams(("parallel",)),
        name="ffn_half",
    )(*ins)


def _rwkv_proj_kernel(x_ref, xprev_ref, shift_ref, nm_ref, mu_ref, wr_ref, wk_ref, wv_ref,
                      w0_ref, w1_ref, w2_ref, a0_ref, a1_ref, a2_ref, g1_ref, g2_ref,
                      kk_ref, ka_ref,
                      r_ref, lw_ref, k_ref, v_ref, as_ref, bs_ref, g_ref, hlast_ref,
                      h_scr, *, tm, tiles_per_seq):
    nm = nm_ref[...]
    h = _rms(x_ref[...], nm, NORM_EPS)
    hlast_ref[0] = h[tm - 1:tm, :]
    first = (pl.program_id(0) % tiles_per_seq) == 0
    h_before = _rms(xprev_ref[7:8, :], nm, NORM_EPS)
    h_scr[8:tm + 8, :] = h
    h_scr[7:8, :] = jnp.where(first, shift_ref[0], h_before)
    xx = h_scr[7:tm + 7, :] - h
    mu = mu_ref[...]

    def mix(m):
        return (h + xx * mu[m:m + 1, :]).astype(bf16)

    r = _dot(mix(0), wr_ref[...])
    k = _dot(mix(2), wk_ref[...])
    v = _dot(mix(3), wv_ref[...])
    ww = w0_ref[...] + _dot(jnp.tanh(_dot(mix(1), w1_ref[...])).astype(bf16), w2_ref[...])
    neg = -ww
    softplus = jnp.maximum(neg, 0.0) + jnp.log(1.0 + jnp.exp(-jnp.abs(neg)))
    lw = -jnp.exp(-softplus - 0.5)
    a = _sigmoid(a0_ref[...] + _dot(_dot(mix(4), a1_ref[...]).astype(bf16), a2_ref[...]))
    g = _dot(_sigmoid(_dot(mix(5), g1_ref[...])).astype(bf16), g2_ref[...])
    kk = k * kk_ref[...]
    norm = jnp.sqrt(_headsum_lanes(kk * kk))
    kk = kk / jnp.maximum(norm, 1e-12)
    r_ref[...] = r
    lw_ref[...] = lw
    k_ref[...] = k * (1.0 + (a - 1.0) * ka_ref[...])
    v_ref[...] = v
    as_ref[...] = -kk
    bs_ref[...] = kk * a
    g_ref[...] = g


def _rwkv_proj(x, shift0, p, tm, tiles_per_seq):
    n, d = x.shape
    nt = n // tm
    row = _rows(tm, d)
    lw_, la_, lg_ = p["w1"].shape[1], p["a1"].shape[1], p["g1"].shape[1]
    specs = [
        row,
        pl.BlockSpec((8, d), lambda i: (jnp.maximum(i * (tm // 8) - 1, 0), 0)),
        pl.BlockSpec((1, 1, d), lambda i: (i // tiles_per_seq, 0, 0)),
        _whole((1, d)), _whole((6, d)),
        _whole((d, d)), _whole((d, d)), _whole((d, d)),
        _whole((1, d)), _whole((d, lw_)), _whole((lw_, d)),
        _whole((1, d)), _whole((d, la_)), _whole((la_, d)),
        _whole((d, lg_)), _whole((lg_, d)),
        _whole((1, d)), _whole((1, d)),
    ]
    outs = [jax.ShapeDtypeStruct((n, d), f32)] * 7 + [jax.ShapeDtypeStruct((nt, 1, d), f32)]
    out_specs = [row] * 7 + [pl.BlockSpec((1, 1, d), lambda i: (i, 0, 0))]
    return pl.pallas_call(
        functools.partial(_rwkv_proj_kernel, tm=tm, tiles_per_seq=tiles_per_seq),
        grid=(nt,),
        in_specs=specs,
        out_specs=out_specs,
        out_shape=outs,
        scratch_shapes=[pltpu.VMEM((tm + 8, d), f32)],
        compiler_params=_cparams(("parallel",)),
        name="rwkv_proj",
    )(x, x, shift0, p["norm_mix"], p["mu"], p["w_r"], p["w_k"], p["w_v"],
      p["w0"], p["w1"], p["w2"], p["a0"], p["a1"], p["a2"], p["g1"], p["g2"],
      p["k_k"], p["k_a"])


def _wkv_kernel(r_ref, lw_ref, k_ref, v_ref, a_ref, b_ref, g_ref, rk_ref, gg_ref, gb_ref, s0_ref,
                z_ref, sT_ref, s_scr, *, L, nb):
    c = pl.program_id(1)
    seqs = range(nb)
    units = range(nb * N_PAIRS)

    @pl.when(c == 0)
    def _():
        zero = jnp.zeros((RW_HEAD, RW_HEAD), f32)
        for u in units:
            b, p = divmod(u, N_PAIRS)
            top = jnp.concatenate([s0_ref[b, HEADS_PER_VREG * p], zero], axis=1)
            bottom = jnp.concatenate([zero, s0_ref[b, HEADS_PER_VREG * p + 1]], axis=1)
            s_scr[u] = jnp.concatenate([top, bottom], axis=0).T

    L2 = 2 * L
    ri = lax.broadcasted_iota(jnp.int32, (L, L), 0)
    ci = lax.broadcasted_iota(jnp.int32, (L, L), 1)
    cum = jnp.where(ri >= ci, 1.0, 0.0).astype(bf16)
    rs = lax.broadcasted_iota(jnp.int32, (L2, LANES), 0)
    ls = lax.broadcasted_iota(jnp.int32, (L2, LANES), 1)
    own = (rs < L) == (ls < RW_HEAD)
    rg = lax.broadcasted_iota(jnp.int32, (L2, L2), 0) & (L - 1)
    cg = lax.broadcasted_iota(jnp.int32, (L2, L2), 1) & (L - 1)
    strict = rg > cg
    incl = rg >= cg

    lw = [lw_ref[b] for b in seqs]
    parts = [_split3(lw[b]) for b in seqs]
    cs = [_dot(cum, parts[b][0]) + _dot(cum, parts[b][1]) + _dot(cum, parts[b][2]) for b in seqs]
    w_in = [jnp.exp(cs[b]) for b in seqs]
    w_inv = [jnp.exp(-cs[b]) for b in seqs]
    pairs = units

    def stack(xs):
        out = []
        for u in units:
            b, p = divmod(u, N_PAIRS)
            xp = xs[b][:, p * LANES:(p + 1) * LANES]
            out.append(jnp.where(own, jnp.concatenate([xp, xp], axis=0), 0.0).astype(bf16))
        return out

    a_st = stack([a_ref[b] * jnp.exp(cs[b] - lw[b]) for b in seqs])
    r_st = stack([r_ref[b] * w_in[b] for b in seqs])
    b_st = stack([b_ref[b] * w_inv[b] for b in seqs])
    k_st = stack([k_ref[b] * w_inv[b] for b in seqs])
    v_st = stack([v_ref[b] for b in seqs])
    gram = [_dot_nt(jnp.concatenate([a_st[p], r_st[p]], axis=0), jnp.concatenate([b_st[p], k_st[p]], axis=0))
            for p in pairs]
    m = [jnp.where(strict, gram[p][:L2, :L2], 0.0) for p in pairs]
    a_ak = [jnp.where(strict, gram[p][:L2, L2:], 0.0).astype(bf16) for p in pairs]
    a_rb = [jnp.where(incl, gram[p][L2:, :L2], 0.0).astype(bf16) for p in pairs]
    a_rk = [jnp.where(incl, gram[p][L2:, L2:], 0.0).astype(bf16) for p in pairs]
    s = [s_scr[p] for p in pairs]
    sb = [s[p].astype(bf16) for p in pairs]

    rhs = [_dot(jnp.concatenate([a_st[p], a_ak[p]], axis=1), jnp.concatenate([sb[p], v_st[p]], axis=0))
           for p in pairs]
    t = m
    mb = [m[p].astype(bf16) for p in pairs]
    m = [_dot(mb[p], mb[p]) for p in pairs]
    n_sq = L.bit_length() - 2
    for k in range(1, n_sq + 1):
        mb = [m[p].astype(bf16) for p in pairs]
        if k < n_sq:
            both = [_dot(mb[p], jnp.concatenate([t[p].astype(bf16), mb[p]], axis=1)) for p in pairs]
            t = [t[p] + m[p] + both[p][:, :L2] for p in pairs]
            m = [both[p][:, L2:] for p in pairs]
        else:
            t = [t[p] + m[p] + _dot(mb[p], t[p].astype(bf16)) for p in pairs]
    ub = [(rhs[p] + _dot(t[p].astype(bf16), rhs[p].astype(bf16))).astype(bf16) for p in pairs]
    y_st = [_dot(jnp.concatenate([r_st[p], a_rb[p], a_rk[p]], axis=1),
                 jnp.concatenate([sb[p], ub[p], v_st[p]], axis=0)) for p in pairs]
    for b in seqs:
        y = jnp.concatenate([y_st[u][:L] + y_st[u][L:] for u in units[b * N_PAIRS:(b + 1) * N_PAIRS]], axis=1)
        d = y - _headsum_lanes(y) * (1.0 / RW_HEAD)
        var = _headsum_lanes(d * d) * (1.0 / RW_HEAD)
        yn = d * lax.rsqrt(var + GN_EPS) * gg_ref[...] + gb_ref[...]
        bonus = _headsum_lanes(r_ref[b] * k_ref[b] * rk_ref[...]) * v_ref[b]
        z_ref[b] = ((yn + bonus) * g_ref[b]).astype(z_ref.dtype)
    eye = (lax.broadcasted_iota(jnp.int32, (LANES, LANES), 0)
           == lax.broadcasted_iota(jnp.int32, (LANES, LANES), 1)).astype(f32)
    for u in units:
        b, p = divmod(u, N_PAIRS)
        w_rows = jnp.sum(eye * w_in[b][L - 1:L, p * LANES:(p + 1) * LANES], axis=1, keepdims=True)
        s_scr[u] = (s[u] + _dot_tn(jnp.concatenate([b_st[u], k_st[u]], axis=0),
                                   jnp.concatenate([ub[u], v_st[u]], axis=0))) * w_rows

    @pl.when(c == pl.num_programs(1) - 1)
    def _():
        for u in units:
            b, p = divmod(u, N_PAIRS)
            s_vk = s_scr[u].T
            sT_ref[b, HEADS_PER_VREG * p] = s_vk[:RW_HEAD, :RW_HEAD]
            sT_ref[b, HEADS_PER_VREG * p + 1] = s_vk[RW_HEAD:, RW_HEAD:]


def _wkv(r, lw, k, v, a, b, g, p, s0, L, nb):
    nseq, t, d = r.shape
    blk = pl.BlockSpec((nb, L, d), lambda s, c: (s, c, 0))
    st = pl.BlockSpec((nb, RW_HEADS, RW_HEAD, RW_HEAD), lambda s, c: (s, 0, 0, 0))
    vec = pl.BlockSpec((1, d), lambda s, c: (0, 0))
    return pl.pallas_call(
        functools.partial(_wkv_kernel, L=L, nb=nb),
        grid=(nseq // nb, t // L),
        in_specs=[blk] * 7 + [vec] * 3 + [st],
        out_specs=[blk, st],
        out_shape=[jax.ShapeDtypeStruct((nseq, t, d), bf16),
                   jax.ShapeDtypeStruct((nseq, RW_HEADS, RW_HEAD, RW_HEAD), f32)],
        scratch_shapes=[pltpu.VMEM((nb * N_PAIRS, LANES, LANES), f32)],
        compiler_params=_cparams(("parallel", "arbitrary")),
        name="wkv_scan",
    )(r, lw, k, v, a, b, g, p["r_k"], p["gn_g"], p["gn_b"], s0)


def _put_rows_kernel(big_ref, rows_ref, o_ref):
    del big_ref
    o_ref[0] = rows_ref[...]


def _put_rows(big, rows):
    n_seq = big.shape[0]
    r = rows.shape[0]
    return pl.pallas_call(
        _put_rows_kernel,
        grid=(n_seq,),
        in_specs=[pl.BlockSpec(memory_space=pl.ANY), pl.BlockSpec((r, LANES), lambda s: (0, 0))],
        out_specs=pl.BlockSpec((1, r, LANES), lambda s: (s, 0, 0)),
        out_shape=jax.ShapeDtypeStruct(big.shape, big.dtype),
        input_output_aliases={0: 0},
        compiler_params=_cparams(("parallel",)),
        name="put_meta_rows",
    )(big, rows)


def _lam(lp):
    e1 = jnp.exp(jnp.sum(lp[0:1, :] * lp[1:2, :], axis=-1, keepdims=True))
    e2 = jnp.exp(jnp.sum(lp[2:3, :] * lp[3:4, :], axis=-1, keepdims=True))
    return e1 - e2 + LAM_INIT


def _split_heads(q):
    lane = lax.broadcasted_iota(jnp.int32, q.shape, 1)
    zero = jnp.zeros_like(q)
    return jnp.where(lane < HEAD_DIM, q, zero), jnp.where(lane >= HEAD_DIM, q, zero)


def _stack_components(q):
    q1, q2 = _split_heads(q)
    return jnp.concatenate([q1, q2], axis=0)


def _with_ones(v):
    return jnp.concatenate([v, jnp.ones(v.shape, v.dtype)], axis=1)


def _attn_finish(acc, lp, sub):
    n = acc.shape[0] // 2
    o = acc[:n, :V_DIM] / acc[:n, V_DIM:] - _lam(lp) * (acc[n:, :V_DIM] / acc[n:, V_DIM:])
    o = o * lax.rsqrt(jnp.mean(o * o, axis=-1, keepdims=True) + SUBLN_EPS) * sub
    return o * (1.0 - LAM_INIT)


def _attn_prompt_kernel(q_ref, k_ref, v_ref, km_ref, vm_ref, bias_ref, lp_ref, sub_ref, *rest, tq, hp, n_tiles):
    o_ref = rest[-1]
    head_cols = [slice(h * V_DIM, (h + 1) * V_DIM) for h in range(hp)]
    qs, head_of = [], []
    for h in range(hp):
        qs += list(_split_heads(q_ref[0, :, head_cols[h]]))
        head_of += [h, h]
    chains = range(len(qs))

    def scores(j):
        kj = [k_ref[0, j * tq:(j + 1) * tq, head_cols[h]] for h in range(hp)]
        return [_dot_nt(qs[c], kj[head_of[c]]) for c in chains]

    def tile(j, s, carry, diagonal):
        vj = [v_ref[0, j * tq:(j + 1) * tq, head_cols[h]] for h in range(hp)]
        if diagonal:
            bias = bias_ref[...]
            s = [s[c] + bias for c in chains]
            sm = [_dot_nt(qs[c], km_ref[:, head_cols[head_of[c]]]) for c in chains]
        m_blk = [jnp.max(s[c], axis=-1, keepdims=True) for c in chains]
        if diagonal:
            m_blk = [jnp.maximum(m_blk[c], jnp.max(sm[c], axis=-1, keepdims=True)) for c in chains]
        m_new = [jnp.maximum(carry[3 * c], m_blk[c]) for c in chains]
        alpha = [jnp.exp2(carry[3 * c] - m_new[c]) for c in chains]
        p = [jnp.exp2(s[c] - m_new[c]) for c in chains]
        l_new = [alpha[c] * carry[3 * c + 1] + jnp.sum(p[c], axis=-1, keepdims=True) for c in chains]
        acc = [alpha[c] * carry[3 * c + 2] + _dot(p[c].astype(bf16), vj[head_of[c]]) for c in chains]
        if diagonal:
            pm = [jnp.exp2(sm[c] - m_new[c]) for c in chains]
            l_new = [l_new[c] + jnp.sum(pm[c], axis=-1, keepdims=True) for c in chains]
            acc = [acc[c] + _dot(pm[c].astype(bf16), vm_ref[:, head_cols[head_of[c]]]) for c in chains]
        out = []
        for c in chains:
            out += [m_new[c], l_new[c], acc[c]]
        return tuple(out)

    init = []
    for _ in chains:
        init += [jnp.full((tq, 1), NEG_INF, f32), jnp.zeros((tq, 1), f32), jnp.zeros((tq, V_DIM), f32)]
    carry = tuple(init)
    s_next = scores(0)
    for j in range(n_tiles):
        s_cur = s_next
        if j + 1 < n_tiles:
            s_next = scores(j + 1)
        carry = tile(j, s_cur, carry, j == n_tiles - 1)
    for h in range(hp):
        acc = jnp.concatenate(
            [jnp.concatenate([carry[3 * c + 2], jnp.broadcast_to(carry[3 * c + 1], (tq, V_DIM))], axis=1)
             for c in (2 * h, 2 * h + 1)], axis=0)
        o_ref[0, :, head_cols[h]] = _attn_finish(acc, lp_ref[...], sub_ref[...]).astype(o_ref.dtype)


def _attn_prompt(q, k, v, km, vm, lp, sub, tq, hp):
    b, t, d = q.shape
    w = hp * V_DIM
    mspec = pl.BlockSpec((N_META, w), lambda bi, h: (0, h))
    chunk_of = jnp.arange(tq) // CHUNK
    bias = jnp.where(chunk_of[None, :] <= chunk_of[:, None], 0.0, NEG_INF).astype(f32)
    out = None
    for i in range(t // tq):
        qspec = pl.BlockSpec((1, tq, w), lambda bi, h, i=i: (bi, i, h))
        kspec = pl.BlockSpec((1, (i + 1) * tq, w), lambda bi, h: (bi, 0, h))
        ins = [q, k, v, km, vm, bias, lp, sub]
        specs = [qspec, kspec, kspec, mspec, mspec,
                 pl.BlockSpec((tq, tq), lambda bi, h: (0, 0), pipeline_mode=pl.Buffered(1)),
                 pl.BlockSpec((4, HEAD_DIM), lambda bi, h: (0, 0)),
                 pl.BlockSpec((1, V_DIM), lambda bi, h: (0, 0))]
        if out is not None:
            ins.append(out)
            specs.append(pl.BlockSpec(memory_space=pl.ANY))
        out = pl.pallas_call(
            functools.partial(_attn_prompt_kernel, tq=tq, hp=hp, n_tiles=i + 1),
            grid=(b, N_HEADS // hp),
            in_specs=specs,
            out_specs=qspec,
            out_shape=jax.ShapeDtypeStruct((b, t, d), bf16),
            input_output_aliases={} if i == 0 else {len(ins) - 1: 0},
            compiler_params=_cparams(("parallel", "parallel")),
            name=f"attn_prompt_{i}",
        )(*ins)
    return out


def _attn_full_kernel(*refs, n_cache):
    if n_cache:
        q_ref, kc_ref, vc_ref, kn_ref, vn_ref, lp_ref, sub_ref, o_ref = refs
    else:
        q_ref, kn_ref, vn_ref, lp_ref, sub_ref, o_ref = refs
    lp = lp_ref[...]
    sub = sub_ref[...]
    for h in range(N_HEADS):
        cols = slice(h * V_DIM, (h + 1) * V_DIM)
        qq = _stack_components(q_ref[0, :, cols])
        keys = [kn_ref[0, :, cols]]
        vals = [vn_ref[0, :, cols]]
        if n_cache:
            for first, count in ((0, N_META), (N_META, n_cache - N_META)):
                rows = pl.ds(first * N_HEADS + h, count, stride=N_HEADS)
                keys.append(kc_ref[0, rows, :].astype(bf16))
                vals.append(vc_ref[0, rows, :].astype(bf16))
        scores = [_dot_nt(qq, kk) for kk in keys]
        m = functools.reduce(jnp.maximum, [jnp.max(s, axis=-1, keepdims=True) for s in scores])
        acc = sum(_dot(jnp.exp2(s - m).astype(bf16), _with_ones(vv)) for s, vv in zip(scores, vals))
        o_ref[0, :, cols] = _attn_finish(acc, lp, sub).astype(o_ref.dtype)


def _attn_full(q, kn, vn, lp, sub, cache=None):
    b, tq, d = q.shape
    spec = pl.BlockSpec((1, tq, d), lambda bi: (bi, 0, 0))
    ins, specs, n_cache = [q], [spec], 0
    if cache is not None:
        n_cache = cache[0].shape[1] // N_HEADS
        cspec = pl.BlockSpec((1, n_cache * N_HEADS, V_DIM), lambda bi: (bi, 0, 0))
        ins += list(cache)
        specs += [cspec, cspec]
    ins += [kn, vn, lp, sub]
    specs += [spec, spec, pl.BlockSpec((4, HEAD_DIM), lambda bi: (0, 0)),
              pl.BlockSpec((1, V_DIM), lambda bi: (0, 0))]
    return pl.pallas_call(
        functools.partial(_attn_full_kernel, n_cache=n_cache),
        grid=(b,),
        in_specs=specs,
        out_specs=spec,
        out_shape=jax.ShapeDtypeStruct((b, tq, d), bf16),
        compiler_params=_cparams(("parallel",)),
        name="attn_full",
    )(*ins)


FRAME_TILE = 512
RWKV_TILE = 256
ATTN_Q_TILE = 512
ATTN_HEADS_PER_STEP = 2
WKV_CHUNK = 64
WKV_SMALL_SEQS_PER_STEP = 3
WKV_SEQS_PER_STEP = 2


def _rope_tables(pos):
    half = ROPE_DIM // 2
    inv = np.power(np.float64(ROPE_THETA), -np.arange(0, ROPE_DIM, 2, dtype=np.float64) / ROPE_DIM)
    ang = pos.astype(np.float64)[:, None] * inv[None, :]
    cos, sin = np.cos(ang).astype(np.float32), np.sin(ang).astype(np.float32)
    n = pos.shape[0]
    one = np.ones((n, HEAD_DIM - ROPE_DIM), np.float32)
    zero = np.zeros((n, HEAD_DIM - ROPE_DIM), np.float32)
    zh = np.zeros((n, half), np.float32)
    c = np.concatenate([cos, cos, one], axis=1)
    sa = np.concatenate([-sin, zh, zero], axis=1)
    sb = np.concatenate([zh, sin, zero], axis=1)
    return tuple(jnp.asarray(np.concatenate([t, t], axis=1), dtype=f32) for t in (c, sa, sb))


def kernel(x_prompt, x_sample, cache_k, cache_v, state_wkv, state_shift, meta_tokens, norm_ffn, norm_mix, ffn_w_gate, ffn_w_up, ffn_w_down, rw_mu, rw_w_r, rw_w_k, rw_w_v, rw_w_o, rw_w0, rw_w1, rw_w2, rw_a0, rw_a1, rw_a2, rw_g1, rw_g2, rw_k_k, rw_k_a, rw_r_k, rw_gn_g, rw_gn_b, kv_norm, w_k, w_v, b_w_q, b_w_o, b_lambda, b_subln, final_norm):
    bp, tp, d = x_prompt.shape
    bd, ts, _ = x_sample.shape
    past = cache_k.shape[1] - N_META
    n_small = N_META + bd * ts
    vec = lambda a: a.reshape(1, -1).astype(f32)
    wb = lambda a: a.astype(bf16)

    rw = dict(norm_mix=vec(norm_mix[0]), mu=rw_mu[0], w_r=wb(rw_w_r[0]), w_k=wb(rw_w_k[0]), w_v=wb(rw_w_v[0]),
              w_o=wb(rw_w_o[0]), w0=vec(rw_w0[0]), w1=wb(rw_w1[0]), w2=wb(rw_w2[0]), a0=vec(rw_a0[0]),
              a1=wb(rw_a1[0]), a2=wb(rw_a2[0]), g1=wb(rw_g1[0]), g2=wb(rw_g2[0]), k_k=vec(rw_k_k[0]),
              k_a=vec(rw_k_a[0]), r_k=vec(rw_r_k[0]), gn_g=vec(rw_gn_g[0]), gn_b=vec(rw_gn_b[0]))
    wg_all, wu_all, wd_all = wb(ffn_w_gate), wb(ffn_w_up), wb(ffn_w_down)
    ffn_half = lambda x, li, j, **kw: _ffn(x, vec(norm_ffn[li, j]), wg_all, wu_all, wd_all, li, j, **kw)
    wk_b, wv_b, wq_b, wo_b = wb(w_k), wb(w_v), wb(b_w_q[0]), wb(b_w_o[0])
    lp = b_lambda[0].astype(f32)
    sub = vec(b_subln[0])
    fin = vec(final_norm)

    seq = N_META
    xs = jnp.concatenate([meta_tokens.astype(f32), x_sample.reshape(bd * ts, d)], axis=0)
    n_seq_s = n_small // seq
    shift_s0 = jnp.concatenate([jnp.zeros((1, d), f32), state_shift[:, 0]], axis=0)[:, None, :]
    wkv_s0 = jnp.concatenate([jnp.zeros((1,) + state_wkv.shape[2:], f32), state_wkv[:, 0]], axis=0)
    pos_s = np.concatenate([np.arange(N_META), np.tile(N_META + past + np.arange(ts), bd)])
    tabs_s = _rope_tables(pos_s)

    (x1s,) = ffn_half(xs, 0, 0, tm=n_small)
    r, lw, k, v, a_s, b_s, g, hl_s = _rwkv_proj(x1s, shift_s0, rw, tm=seq, tiles_per_seq=1)
    sq = lambda z: z.reshape(n_seq_s, seq, d)
    z, st_s = _wkv(sq(r), sq(lw), sq(k), sq(v), sq(a_s), sq(b_s), sq(g), rw, wkv_s0, L=seq, nb=WKV_SMALL_SEQS_PER_STEP)
    x3s, k_s, v_s, kb_s, vb_s = ffn_half(x1s, 0, 1, tm=n_small, attn=(z.reshape(n_small, d), rw["w_o"]),
                                         kv=(vec(kv_norm), wk_b, wv_b, tabs_s, 1, None))
    x4s, q_s = ffn_half(x3s, 1, 0, tm=n_small, q=(vec(norm_mix[1]), wq_b, tabs_s, 1))
    o_meta = _attn_full(q_s[None, :N_META], kb_s[None, :N_META], vb_s[None, :N_META], lp, sub)
    bs3 = lambda z: z[N_META:].reshape(bd, ts, d)
    cache_rows = lambda z: z.reshape(bd, (N_META + past) * N_HEADS, V_DIM)
    o_samp = _attn_full(bs3(q_s), bs3(kb_s), bs3(vb_s), lp, sub, cache=(cache_rows(cache_k), cache_rows(cache_v)))
    o_s = jnp.concatenate([o_meta[0], o_samp.reshape(bd * ts, d)], axis=0)
    (ys,) = ffn_half(x4s, 1, 1, tm=n_small, attn=(o_s, wo_b), final_g=fin)

    n_f = bp * tp
    xf = x_prompt.reshape(n_f, d)
    tabs_f = _rope_tables(N_META + np.arange(tp))
    shift_f0 = jnp.broadcast_to(hl_s[0:1], (bp, 1, d))
    wkv_f0 = jnp.broadcast_to(st_s[0:1], (bp,) + st_s.shape[1:])

    (x1,) = ffn_half(xf, 0, 0, tm=FRAME_TILE)
    r, lw, k, v, a_s, b_s, g, hl_f = _rwkv_proj(x1, shift_f0, rw, tm=RWKV_TILE, tiles_per_seq=tp // RWKV_TILE)
    sq = lambda z: z.reshape(bp, tp, d)
    z, st_f = _wkv(sq(r), sq(lw), sq(k), sq(v), sq(a_s), sq(b_s), sq(g), rw, wkv_f0, L=WKV_CHUNK, nb=WKV_SEQS_PER_STEP)
    tab_blocks = tp // FRAME_TILE
    x3, k_f, v_f, kb_f, vb_f = ffn_half(
        x1, 0, 1, tm=FRAME_TILE, attn=(z.reshape(n_f, d), rw["w_o"]),
        kv=(vec(kv_norm), wk_b, wv_b, tabs_f, tab_blocks, (bp, N_META + tp, N_META)))
    x4, q_f = ffn_half(x3, 1, 0, tm=FRAME_TILE, q=(vec(norm_mix[1]), wq_b, tabs_f, tab_blocks))
    o_f = _attn_prompt(sq(q_f), sq(kb_f), sq(vb_f), kb_s[:N_META], vb_s[:N_META], lp, sub, tq=ATTN_Q_TILE,
                       hp=ATTN_HEADS_PER_STEP)
    (yf,) = ffn_half(x4, 1, 1, tm=FRAME_TILE, attn=(o_f.reshape(n_f, d), wo_b), final_g=fin)

    y_prompt = yf.reshape(bp, tp, d)
    y_sample = ys[N_META:].reshape(bd, ts, d)
    wkv_p = st_f[:, None]
    wkv_s = st_s[1:][:, None]
    last_tile = tp // RWKV_TILE - 1
    shift_p = hl_f.reshape(bp, tp // RWKV_TILE, d)[:, last_tile][:, None]
    shift_s = hl_s[1:, 0][:, None]
    meta_rows = lambda z: z[:N_META].reshape(N_META * N_HEADS, V_DIM)
    new_k_p = _put_rows(k_f, meta_rows(k_s)).reshape(bp, N_META + tp, N_HEADS, V_DIM)
    new_v_p = _put_rows(v_f, meta_rows(v_s)).reshape(bp, N_META + tp, N_HEADS, V_DIM)
    new_k_s = k_s[N_META:].reshape(bd, ts, N_HEADS, V_DIM)
    new_v_s = v_s[N_META:].reshape(bd, ts, N_HEADS, V_DIM)
    return (y_prompt, y_sample, wkv_p, shift_p, new_k_p, new_v_p, wkv_s, shift_s, new_k_s, new_v_s)
```

```python
import functools
import math

import jax
import jax.numpy as jnp
import numpy as np
from jax import lax
from jax.experimental import pallas as pl
from jax.experimental.pallas import tpu as pltpu

f32 = jnp.float32
bf16 = jnp.bfloat16

D_MODEL = 1024
CHUNK = 64
N_META = 16
RW_HEAD = 64
RW_HEADS = D_MODEL // RW_HEAD
GN_EPS = 64e-5
HEAD_DIM = 64
N_HEADS = D_MODEL // (2 * HEAD_DIM)
V_DIM = 2 * HEAD_DIM
ROPE_DIM = HEAD_DIM // 4
ROPE_THETA = 500000.0
SUBLN_EPS = 1e-5
NEG_INF = -1e30
NORM_EPS = 1e-6
LAM_INIT = 0.8 - 0.6 * math.exp(-0.3 * 1)
LOG2E = math.log2(math.e)
CHUNK_SHIFT = CHUNK.bit_length() - 1

LANES = 128
HEADS_PER_VREG = LANES // RW_HEAD
N_PAIRS = D_MODEL // LANES
VMEM_LIMIT = 56 * 1024 * 1024


def _cparams(sem):
    return pltpu.CompilerParams(dimension_semantics=sem, vmem_limit_bytes=VMEM_LIMIT)


def _dot(a, b):
    return jnp.dot(a, b, preferred_element_type=f32)


def _dot_nt(a, b):
    return lax.dot_general(a, b, (((1,), (1,)), ((), ())), preferred_element_type=f32)


def _dot_tn(a, b):
    return lax.dot_general(a, b, (((0,), (0,)), ((), ())), preferred_element_type=f32)


def _split3(x):
    h1 = x.astype(bf16)
    r1 = x - h1.astype(f32)
    h2 = r1.astype(bf16)
    h3 = (r1 - h2.astype(f32)).astype(bf16)
    return h1, h2, h3


def _sigmoid(x):
    return 1.0 / (1.0 + jnp.exp(-x))


def _rms(x, g, eps):
    return x * lax.rsqrt(jnp.mean(x * x, axis=-1, keepdims=True) + eps) * g


def _headsum_lanes(x):
    first = lax.broadcasted_iota(jnp.int32, (x.shape[0], LANES), 1) < RW_HEAD
    out = []
    for j in range(x.shape[1] // LANES):
        xs = x[:, j * LANES:(j + 1) * LANES]
        both = jnp.sum(xs, axis=-1, keepdims=True)
        head_a = jnp.sum(jnp.where(first, xs, 0.0), axis=-1, keepdims=True)
        out.append(jnp.where(first, head_a, both - head_a))
    return jnp.concatenate(out, axis=1)


def _rope(x, cos, sin_a, sin_b):
    reps = x.shape[1] // LANES
    c = jnp.concatenate([cos] * reps, axis=1)
    sa = jnp.concatenate([sin_a] * reps, axis=1)
    sb = jnp.concatenate([sin_b] * reps, axis=1)
    half = ROPE_DIM // 2
    return x * c + pltpu.roll(x, x.shape[1] - half, 1) * sa + pltpu.roll(x, half, 1) * sb


def _row(i):
    return (i, 0)


def _fixed2(i):
    return (0, 0)


def _rows(tm, w):
    return pl.BlockSpec((tm, w), _row)


def _whole(shape):
    return pl.BlockSpec(shape, _fixed2, pipeline_mode=pl.Buffered(1))


def _ffn_kernel(*refs, attn_in, final_norm, emit_q, emit_kv, cache_layout):
    it = iter(refs)
    x_ref, g_ref, wg_ref, wu_ref, wd_ref = (next(it) for _ in range(5))
    x = x_ref[...]
    if attn_in:
        o_in_ref, wo_ref = next(it), next(it)
        x = x + _dot(o_in_ref[...], wo_ref[...])
    xb = _rms(x, g_ref[...], NORM_EPS).astype(bf16)
    gate = _dot(xb, wg_ref[...])
    up = _dot(xb, wu_ref[...])
    act = (gate * _sigmoid(gate) * up).astype(bf16)
    y = x + 0.5 * _dot(act, wd_ref[...])
    if final_norm:
        y = _rms(y, next(it)[...], NORM_EPS)
    if emit_q:
        nq_ref, wq_ref, cos_ref, sa_ref, sb_ref = (next(it) for _ in range(5))
    if emit_kv:
        nkv_ref, wk_ref, wv_ref, cos_ref, sa_ref, sb_ref = (next(it) for _ in range(6))
    next(it)[...] = y
    if emit_q:
        hb = _rms(y, nq_ref[...], NORM_EPS).astype(bf16)
        q = _rope(_dot(hb, wq_ref[...]), cos_ref[...], sa_ref[...], sb_ref[...])
        next(it)[...] = (q * (HEAD_DIM ** -0.5 * LOG2E)).astype(bf16)
    if emit_kv:
        k_ref, v_ref, kb_ref, vb_ref = (next(it) for _ in range(4))
        hb = _rms(y, nkv_ref[...], NORM_EPS).astype(bf16)
        k = _rope(_dot(hb, wk_ref[...]), cos_ref[...], sa_ref[...], sb_ref[...])
        v = _dot(hb, wv_ref[...])
        kb_ref[...] = k.astype(bf16)
        vb_ref[...] = v.astype(bf16)
        if cache_layout:
            tm = k.shape[0]
            for h in range(N_HEADS):
                k_ref[0, pl.ds(h, tm, stride=N_HEADS), :] = k[:, h * V_DIM:(h + 1) * V_DIM]
                v_ref[0, pl.ds(h, tm, stride=N_HEADS), :] = v[:, h * V_DIM:(h + 1) * V_DIM]
        else:
            k_ref[...] = k
            v_ref[...] = v


def _ffn(x, g, wg, wu, wd, layer, half, tm, attn=None, final_g=None, q=None, kv=None):
    n, d = x.shape
    dff = wg.shape[-1]
    row = _rows(tm, d)

    def pick(rows, cols):
        return pl.BlockSpec((None, None, rows, cols), lambda i: (layer, half, 0, 0), pipeline_mode=pl.Buffered(1))

    ins = [x, g, wg, wu, wd]
    specs = [row, _whole((1, d)), pick(d, dff), pick(d, dff), pick(dff, d)]
    outs = [jax.ShapeDtypeStruct((n, d), f32)]
    out_specs = [row]
    if attn is not None:
        ins += list(attn)
        specs += [row, _whole((d, d))]
    if final_g is not None:
        ins.append(final_g)
        specs.append(_whole((1, d)))
    cache_rows = None
    if q is not None or kv is not None:
        *weights, tabs, tab_blocks = (q if q is not None else kv[:-1])
        tab = pl.BlockSpec((tm, LANES), lambda i: (i % tab_blocks, 0))
        ins += list(weights) + list(tabs)
        specs += [_whole((1, d))] + [_whole((d, d))] * (len(weights) - 1) + [tab] * 3
    if q is not None:
        outs.append(jax.ShapeDtypeStruct((n, d), bf16))
        out_specs.append(row)
    if kv is not None:
        cache_rows = kv[-1]
        if cache_rows is None:
            f32_spec, f32_shape = row, jax.ShapeDtypeStruct((n, d), f32)
        else:
            n_seq, rows_per_seq, first_row = cache_rows
            tiles_per_seq = n // (n_seq * tm)
            f32_spec = pl.BlockSpec(
                (pl.Element(1), pl.Element(tm * N_HEADS), pl.Element(V_DIM)),
                lambda i: (i // tiles_per_seq, (first_row + (i % tiles_per_seq) * tm) * N_HEADS, 0))
            f32_shape = jax.ShapeDtypeStruct((n_seq, rows_per_seq * N_HEADS, V_DIM), f32)
        outs += [f32_shape] * 2 + [jax.ShapeDtypeStruct((n, d), bf16)] * 2
        out_specs += [f32_spec, f32_spec, row, row]
    return pl.pallas_call(
        functools.partial(_ffn_kernel, attn_in=attn is not None, final_norm=final_g is not None,
                          emit_q=q is not None, emit_kv=kv is not None, cache_layout=cache_rows is not None),
        grid=(n // tm,),
        in_specs=specs,
        out_specs=out_specs,
        out_shape=outs,
        compiler_params=_cparams(("parallel",)),
        name="ffn_half",
    )(*ins)


def _rwkv_proj_kernel(x_ref, xprev_ref, shift_ref, nm_ref, mu_ref, wr_ref, wk_ref, wv_ref,
                      w0_ref, w1_ref, w2_ref, a0_ref, a1_ref, a2_ref, g1_ref, g2_ref,
                      kk_ref, ka_ref,
                      r_ref, lw_ref, k_ref, v_ref, as_ref, bs_ref, g_ref, hlast_ref,
                      h_scr, *, tm, tiles_per_seq):
    nm = nm_ref[...]
    h = _rms(x_ref[...], nm, NORM_EPS)
    hlast_ref[0] = h[tm - 1:tm, :]
    first = (pl.program_id(0) % tiles_per_seq) == 0
    h_before = _rms(xprev_ref[7:8, :], nm, NORM_EPS)
    h_scr[8:tm + 8, :] = h
    h_scr[7:8, :] = jnp.where(first, shift_ref[0], h_before)
    xx = h_scr[7:tm + 7, :] - h
    mu = mu_ref[...]

    def mix(m):
        return (h + xx * mu[m:m + 1, :]).astype(bf16)

    r = _dot(mix(0), wr_ref[...])
    k = _dot(mix(2), wk_ref[...])
    v = _dot(mix(3), wv_ref[...])
    ww = w0_ref[...] + _dot(jnp.tanh(_dot(mix(1), w1_ref[...])).astype(bf16), w2_ref[...])
    lw = -math.exp(-0.5) * _sigmoid(ww)
    a = _sigmoid(a0_ref[...] + _dot(_dot(mix(4), a1_ref[...]).astype(bf16), a2_ref[...]))
    g = _dot(_sigmoid(_dot(mix(5), g1_ref[...])).astype(bf16), g2_ref[...])
    kk = k * kk_ref[...]
    norm = jnp.sqrt(_headsum_lanes(kk * kk))
    kk = kk / jnp.maximum(norm, 1e-12)
    r_ref[...] = r
    lw_ref[...] = lw
    k_ref[...] = k * (1.0 + (a - 1.0) * ka_ref[...])
    v_ref[...] = v
    as_ref[...] = -kk
    bs_ref[...] = kk * a
    g_ref[...] = g


def _rwkv_proj(x, shift0, p, tm, tiles_per_seq):
    n, d = x.shape
    nt = n // tm
    row = _rows(tm, d)
    lw_, la_, lg_ = p["w1"].shape[1], p["a1"].shape[1], p["g1"].shape[1]
    specs = [
        row,
        pl.BlockSpec((8, d), lambda i: (jnp.maximum(i * (tm // 8) - 1, 0), 0)),
        pl.BlockSpec((1, 1, d), lambda i: (i // tiles_per_seq, 0, 0)),
        _whole((1, d)), _whole((6, d)),
        _whole((d, d)), _whole((d, d)), _whole((d, d)),
        _whole((1, d)), _whole((d, lw_)), _whole((lw_, d)),
        _whole((1, d)), _whole((d, la_)), _whole((la_, d)),
        _whole((d, lg_)), _whole((lg_, d)),
        _whole((1, d)), _whole((1, d)),
    ]
    outs = [jax.ShapeDtypeStruct((n, d), f32)] * 7 + [jax.ShapeDtypeStruct((nt, 1, d), f32)]
    out_specs = [row] * 7 + [pl.BlockSpec((1, 1, d), lambda i: (i, 0, 0))]
    return pl.pallas_call(
        functools.partial(_rwkv_proj_kernel, tm=tm, tiles_per_seq=tiles_per_seq),
        grid=(nt,),
        in_specs=specs,
        out_specs=out_specs,
        out_shape=outs,
        scratch_shapes=[pltpu.VMEM((tm + 8, d), f32)],
        compiler_params=_cparams(("parallel",)),
        name="rwkv_proj",
    )(x, x, shift0, p["norm_mix"], p["mu"], p["w_r"], p["w_k"], p["w_v"],
      p["w0"], p["w1"], p["w2"], p["a0"], p["a1"], p["a2"], p["g1"], p["g2"],
      p["k_k"], p["k_a"])


def _wkv_kernel(r_ref, lw_ref, k_ref, v_ref, a_ref, b_ref, g_ref, rk_ref, gg_ref, gb_ref, s0_ref,
                z_ref, sT_ref, s_scr, *, L, nb):
    c = pl.program_id(1)
    seqs = range(nb)
    units = range(nb * N_PAIRS)

    @pl.when(c == 0)
    def _():
        zero = jnp.zeros((RW_HEAD, RW_HEAD), f32)
        for u in units:
            b, p = divmod(u, N_PAIRS)
            top = jnp.concatenate([s0_ref[b, HEADS_PER_VREG * p], zero], axis=1)
            bottom = jnp.concatenate([zero, s0_ref[b, HEADS_PER_VREG * p + 1]], axis=1)
            s_scr[u] = jnp.concatenate([top, bottom], axis=0).T

    L2 = 2 * L
    ri = lax.broadcasted_iota(jnp.int32, (L, L), 0)
    ci = lax.broadcasted_iota(jnp.int32, (L, L), 1)
    cum = jnp.where(ri >= ci, 1.0, 0.0).astype(bf16)
    rs = lax.broadcasted_iota(jnp.int32, (L2, LANES), 0)
    ls = lax.broadcasted_iota(jnp.int32, (L2, LANES), 1)
    own = (rs < L) == (ls < RW_HEAD)
    rg = lax.broadcasted_iota(jnp.int32, (L2, L2), 0) & (L - 1)
    cg = lax.broadcasted_iota(jnp.int32, (L2, L2), 1) & (L - 1)
    strict = rg > cg
    incl = rg >= cg

    lw = [lw_ref[b] for b in seqs]
    parts = [_split3(lw[b]) for b in seqs]
    cs = [_dot(cum, parts[b][0]) + _dot(cum, parts[b][1]) + _dot(cum, parts[b][2]) for b in seqs]
    w_in = [jnp.exp(cs[b]) for b in seqs]
    w_inv = [jnp.exp(-cs[b]) for b in seqs]
    pairs = units

    def stack(xs):
        out = []
        for u in units:
            b, p = divmod(u, N_PAIRS)
            xp = xs[b][:, p * LANES:(p + 1) * LANES]
            out.append(jnp.where(own, jnp.concatenate([xp, xp], axis=0), 0.0).astype(bf16))
        return out

    a_st = stack([a_ref[b] * jnp.exp(cs[b] - lw[b]) for b in seqs])
    r_st = stack([r_ref[b] * w_in[b] for b in seqs])
    b_st = stack([b_ref[b] * w_inv[b] for b in seqs])
    k_st = stack([k_ref[b] * w_inv[b] for b in seqs])
    v_st = stack([v_ref[b] for b in seqs])
    gram = [_dot_nt(jnp.concatenate([a_st[p], r_st[p]], axis=0), jnp.concatenate([b_st[p], k_st[p]], axis=0))
            for p in pairs]
    m = [jnp.where(strict, gram[p][:L2, :L2], 0.0) for p in pairs]
    a_ak = [jnp.where(strict, gram[p][:L2, L2:], 0.0).astype(bf16) for p in pairs]
    a_rb = [jnp.where(incl, gram[p][L2:, :L2], 0.0).astype(bf16) for p in pairs]
    a_rk = [jnp.where(incl, gram[p][L2:, L2:], 0.0).astype(bf16) for p in pairs]
    s = [s_scr[p] for p in pairs]
    sb = [s[p].astype(bf16) for p in pairs]

    rhs = [_dot(jnp.concatenate([a_st[p], a_ak[p]], axis=1), jnp.concatenate([sb[p], v_st[p]], axis=0))
           for p in pairs]
    t = m
    mb = [m[p].astype(bf16) for p in pairs]
    m = [_dot(mb[p], mb[p]) for p in pairs]
    n_sq = L.bit_length() - 2
    for k in range(1, n_sq + 1):
        mb = [m[p].astype(bf16) for p in pairs]
        if k < n_sq:
            both = [_dot(mb[p], jnp.concatenate([t[p].astype(bf16), mb[p]], axis=1)) for p in pairs]
            t = [t[p] + m[p] + both[p][:, :L2] for p in pairs]
            m = [both[p][:, L2:] for p in pairs]
        else:
            t = [t[p] + m[p] + _dot(mb[p], t[p].astype(bf16)) for p in pairs]
    ub = [(rhs[p] + _dot(t[p].astype(bf16), rhs[p].astype(bf16))).astype(bf16) for p in pairs]
    y_st = [_dot(jnp.concatenate([r_st[p], a_rb[p], a_rk[p]], axis=1),
                 jnp.concatenate([sb[p], ub[p], v_st[p]], axis=0)) for p in pairs]
    for b in seqs:
        y = jnp.concatenate([y_st[u][:L] + y_st[u][L:] for u in units[b * N_PAIRS:(b + 1) * N_PAIRS]], axis=1)
        d = y - _headsum_lanes(y) * (1.0 / RW_HEAD)
        var = _headsum_lanes(d * d) * (1.0 / RW_HEAD)
        yn = d * lax.rsqrt(var + GN_EPS) * gg_ref[...] + gb_ref[...]
        bonus = _headsum_lanes(r_ref[b] * k_ref[b] * rk_ref[...]) * v_ref[b]
        z_ref[b] = ((yn + bonus) * g_ref[b]).astype(z_ref.dtype)
    eye = (lax.broadcasted_iota(jnp.int32, (LANES, LANES), 0)
           == lax.broadcasted_iota(jnp.int32, (LANES, LANES), 1)).astype(f32)
    for u in units:
        b, p = divmod(u, N_PAIRS)
        w_rows = jnp.sum(eye * w_in[b][L - 1:L, p * LANES:(p + 1) * LANES], axis=1, keepdims=True)
        s_scr[u] = (s[u] + _dot_tn(jnp.concatenate([b_st[u], k_st[u]], axis=0),
                                   jnp.concatenate([ub[u], v_st[u]], axis=0))) * w_rows

    @pl.when(c == pl.num_programs(1) - 1)
    def _():
        for u in units:
            b, p = divmod(u, N_PAIRS)
            s_vk = s_scr[u].T
            sT_ref[b, HEADS_PER_VREG * p] = s_vk[:RW_HEAD, :RW_HEAD]
            sT_ref[b, HEADS_PER_VREG * p + 1] = s_vk[RW_HEAD:, RW_HEAD:]


def _wkv(r, lw, k, v, a, b, g, p, s0, L, nb):
    nseq, t, d = r.shape
    blk = pl.BlockSpec((nb, L, d), lambda s, c: (s, c, 0))
    st = pl.BlockSpec((nb, RW_HEADS, RW_HEAD, RW_HEAD), lambda s, c: (s, 0, 0, 0))
    vec = pl.BlockSpec((1, d), lambda s, c: (0, 0))
    return pl.pallas_call(
        functools.partial(_wkv_kernel, L=L, nb=nb),
        grid=(nseq // nb, t // L),
        in_specs=[blk] * 7 + [vec] * 3 + [st],
        out_specs=[blk, st],
        out_shape=[jax.ShapeDtypeStruct((nseq, t, d), bf16),
                   jax.ShapeDtypeStruct((nseq, RW_HEADS, RW_HEAD, RW_HEAD), f32)],
        scratch_shapes=[pltpu.VMEM((nb * N_PAIRS, LANES, LANES), f32)],
        compiler_params=_cparams(("parallel", "arbitrary")),
        name="wkv_scan",
    )(r, lw, k, v, a, b, g, p["r_k"], p["gn_g"], p["gn_b"], s0)


def _put_rows_kernel(big_ref, rows_ref, o_ref):
    del big_ref
    o_ref[0] = rows_ref[...]


def _put_rows(big, rows):
    n_seq = big.shape[0]
    r = rows.shape[0]
    return pl.pallas_call(
        _put_rows_kernel,
        grid=(n_seq,),
        in_specs=[pl.BlockSpec(memory_space=pl.ANY), pl.BlockSpec((r, LANES), lambda s: (0, 0))],
        out_specs=pl.BlockSpec((1, r, LANES), lambda s: (s, 0, 0)),
        out_shape=jax.ShapeDtypeStruct(big.shape, big.dtype),
        input_output_aliases={0: 0},
        compiler_params=_cparams(("parallel",)),
        name="put_meta_rows",
    )(big, rows)


def _lam(lp):
    e1 = jnp.exp(jnp.sum(lp[0:1, :] * lp[1:2, :], axis=-1, keepdims=True))
    e2 = jnp.exp(jnp.sum(lp[2:3, :] * lp[3:4, :], axis=-1, keepdims=True))
    return e1 - e2 + LAM_INIT


def _split_heads(q):
    lane = lax.broadcasted_iota(jnp.int32, q.shape, 1)
    zero = jnp.zeros_like(q)
    return jnp.where(lane < HEAD_DIM, q, zero), jnp.where(lane >= HEAD_DIM, q, zero)


def _stack_components(q):
    q1, q2 = _split_heads(q)
    return jnp.concatenate([q1, q2], axis=0)


def _with_ones(v):
    return jnp.concatenate([v, jnp.ones(v.shape, v.dtype)], axis=1)


def _attn_finish(acc, lp, sub):
    n = acc.shape[0] // 2
    o = acc[:n, :V_DIM] / acc[:n, V_DIM:] - _lam(lp) * (acc[n:, :V_DIM] / acc[n:, V_DIM:])
    o = o * lax.rsqrt(jnp.mean(o * o, axis=-1, keepdims=True) + SUBLN_EPS) * sub
    return o * (1.0 - LAM_INIT)


def _attn_prompt_kernel(q_ref, k_ref, v_ref, km_ref, vm_ref, bias_ref, lp_ref, sub_ref, *rest, tq, hp, n_tiles):
    o_ref = rest[-1]
    head_cols = [slice(h * V_DIM, (h + 1) * V_DIM) for h in range(hp)]
    qs, head_of = [], []
    for h in range(hp):
        qs += list(_split_heads(q_ref[0, :, head_cols[h]]))
        head_of += [h, h]
    chains = range(len(qs))

    half = tq // 2

    def spans(diagonal):
        return [(0, half, half), (half, tq, tq)] if diagonal else [(0, tq, tq)]

    def scores(j, diagonal):
        kj = [k_ref[0, j * tq:(j + 1) * tq, head_cols[h]] for h in range(hp)]
        return [_dot_nt(qs[c][r0:r1], kj[head_of[c]][:nk]) for c in chains for r0, r1, nk in spans(diagonal)]

    def tile(j, s, carry, diagonal):
        sp = spans(diagonal)
        units = [(c, r0, r1, nk) for c in chains for r0, r1, nk in sp]
        idx = range(len(units))
        vj = [_with_ones(v_ref[0, j * tq:(j + 1) * tq, head_cols[h]]) for h in range(hp)]
        m_old = [carry[2 * c][r0:r1] for c, r0, r1, _ in units]
        acc_old = [carry[2 * c + 1][r0:r1] for c, r0, r1, _ in units]
        if diagonal:
            bias = bias_ref[...]
            s = [s[u] + bias[units[u][1]:units[u][2], :units[u][3]] for u in idx]
            sm = [_dot_nt(qs[c][r0:r1], km_ref[:, head_cols[head_of[c]]]) for c, r0, r1, _ in units]
        m_blk = [jnp.max(s[u], axis=-1, keepdims=True) for u in idx]
        if diagonal:
            m_blk = [jnp.maximum(m_blk[u], jnp.max(sm[u], axis=-1, keepdims=True)) for u in idx]
        m_new = [jnp.maximum(m_old[u], m_blk[u]) for u in idx]
        alpha = [jnp.exp2(m_old[u] - m_new[u]) for u in idx]
        p = [jnp.exp2((s[u] - m_new[u]).astype(bf16)) for u in idx]
        acc = [alpha[u] * acc_old[u] + _dot(p[u], vj[head_of[units[u][0]]][:units[u][3]]) for u in idx]
        if diagonal:
            pm = [jnp.exp2((sm[u] - m_new[u]).astype(bf16)) for u in idx]
            vm = [_with_ones(vm_ref[:, head_cols[h]]) for h in range(hp)]
            acc = [acc[u] + _dot(pm[u], vm[head_of[units[u][0]]]) for u in idx]
        out = []
        for c in chains:
            mine = [u for u in idx if units[u][0] == c]
            out += [jnp.concatenate([m_new[u] for u in mine], axis=0) if len(mine) > 1 else m_new[mine[0]],
                    jnp.concatenate([acc[u] for u in mine], axis=0) if len(mine) > 1 else acc[mine[0]]]
        return tuple(out)

    init = []
    for _ in chains:
        init += [jnp.full((tq, 1), NEG_INF, f32), jnp.zeros((tq, 2 * V_DIM), f32)]
    carry = tuple(init)
    s_next = scores(0, n_tiles == 1)
    for j in range(n_tiles):
        s_cur = s_next
        if j + 1 < n_tiles:
            s_next = scores(j + 1, j + 2 == n_tiles)
        carry = tile(j, s_cur, carry, j == n_tiles - 1)
    for h in range(hp):
        acc = jnp.concatenate([carry[2 * c + 1] for c in (2 * h, 2 * h + 1)], axis=0)
        o_ref[0, :, head_cols[h]] = _attn_finish(acc, lp_ref[...], sub_ref[...]).astype(o_ref.dtype)


def _attn_prompt(q, k, v, km, vm, lp, sub, tq, hp):
    b, t, d = q.shape
    w = hp * V_DIM
    mspec = pl.BlockSpec((N_META, w), lambda bi, h: (0, h))
    chunk_of = jnp.arange(tq) // CHUNK
    bias = jnp.where(chunk_of[None, :] <= chunk_of[:, None], 0.0, NEG_INF).astype(f32)
    out = None
    for i in range(t // tq):
        qspec = pl.BlockSpec((1, tq, w), lambda bi, h, i=i: (bi, i, h))
        kspec = pl.BlockSpec((1, (i + 1) * tq, w), lambda bi, h: (bi, 0, h))
        ins = [q, k, v, km, vm, bias, lp, sub]
        specs = [qspec, kspec, kspec, mspec, mspec,
                 pl.BlockSpec((tq, tq), lambda bi, h: (0, 0), pipeline_mode=pl.Buffered(1)),
                 pl.BlockSpec((4, HEAD_DIM), lambda bi, h: (0, 0)),
                 pl.BlockSpec((1, V_DIM), lambda bi, h: (0, 0))]
        if out is not None:
            ins.append(out)
            specs.append(pl.BlockSpec(memory_space=pl.ANY))
        out = pl.pallas_call(
            functools.partial(_attn_prompt_kernel, tq=tq, hp=hp, n_tiles=i + 1),
            grid=(b, N_HEADS // hp),
            in_specs=specs,
            out_specs=qspec,
            out_shape=jax.ShapeDtypeStruct((b, t, d), bf16),
            input_output_aliases={} if i == 0 else {len(ins) - 1: 0},
            compiler_params=_cparams(("parallel", "parallel")),
            name=f"attn_prompt_{i}",
        )(*ins)
    return out


def _attn_full_kernel(*refs, n_cache):
    if n_cache:
        q_ref, kc_ref, vc_ref, kn_ref, vn_ref, lp_ref, sub_ref, o_ref = refs
    else:
        q_ref, kn_ref, vn_ref, lp_ref, sub_ref, o_ref = refs
    lp = lp_ref[...]
    sub = sub_ref[...]
    for h in range(N_HEADS):
        cols = slice(h * V_DIM, (h + 1) * V_DIM)
        qq = _stack_components(q_ref[0, :, cols])
        keys = [kn_ref[0, :, cols]]
        vals = [vn_ref[0, :, cols]]
        if n_cache:
            for first, count in ((0, N_META), (N_META, n_cache - N_META)):
                rows = pl.ds(first * N_HEADS + h, count, stride=N_HEADS)
                keys.append(kc_ref[0, rows, :].astype(bf16))
                vals.append(vc_ref[0, rows, :].astype(bf16))
        scores = [_dot_nt(qq, kk) for kk in keys]
        m = functools.reduce(jnp.maximum, [jnp.max(s, axis=-1, keepdims=True) for s in scores])
        acc = sum(_dot(jnp.exp2(s - m).astype(bf16), _with_ones(vv)) for s, vv in zip(scores, vals))
        o_ref[0, :, cols] = _attn_finish(acc, lp, sub).astype(o_ref.dtype)


def _attn_full(q, kn, vn, lp, sub, cache=None):
    b, tq, d = q.shape
    spec = pl.BlockSpec((1, tq, d), lambda bi: (bi, 0, 0))
    ins, specs, n_cache = [q], [spec], 0
    if cache is not None:
        n_cache = cache[0].shape[1] // N_HEADS
        cspec = pl.BlockSpec((1, n_cache * N_HEADS, V_DIM), lambda bi: (bi, 0, 0))
        ins += list(cache)
        specs += [cspec, cspec]
    ins += [kn, vn, lp, sub]
    specs += [spec, spec, pl.BlockSpec((4, HEAD_DIM), lambda bi: (0, 0)),
              pl.BlockSpec((1, V_DIM), lambda bi: (0, 0))]
    return pl.pallas_call(
        functools.partial(_attn_full_kernel, n_cache=n_cache),
        grid=(b,),
        in_specs=specs,
        out_specs=spec,
        out_shape=jax.ShapeDtypeStruct((b, tq, d), bf16),
        compiler_params=_cparams(("parallel",)),
        name="attn_full",
    )(*ins)


FRAME_TILE = 512
RWKV_TILE = 256
ATTN_Q_TILE = 512
ATTN_HEADS_PER_STEP = 2
WKV_CHUNK = 64
WKV_SMALL_SEQS_PER_STEP = 3
WKV_SEQS_PER_STEP = 2


def _rope_tables(pos):
    half = ROPE_DIM // 2
    inv = np.power(np.float64(ROPE_THETA), -np.arange(0, ROPE_DIM, 2, dtype=np.float64) / ROPE_DIM)
    ang = pos.astype(np.float64)[:, None] * inv[None, :]
    cos, sin = np.cos(ang).astype(np.float32), np.sin(ang).astype(np.float32)
    n = pos.shape[0]
    one = np.ones((n, HEAD_DIM - ROPE_DIM), np.float32)
    zero = np.zeros((n, HEAD_DIM - ROPE_DIM), np.float32)
    zh = np.zeros((n, half), np.float32)
    c = np.concatenate([cos, cos, one], axis=1)
    sa = np.concatenate([-sin, zh, zero], axis=1)
    sb = np.concatenate([zh, sin, zero], axis=1)
    return tuple(jnp.asarray(np.concatenate([t, t], axis=1), dtype=f32) for t in (c, sa, sb))


def kernel(x_prompt, x_sample, cache_k, cache_v, state_wkv, state_shift, meta_tokens, norm_ffn, norm_mix, ffn_w_gate, ffn_w_up, ffn_w_down, rw_mu, rw_w_r, rw_w_k, rw_w_v, rw_w_o, rw_w0, rw_w1, rw_w2, rw_a0, rw_a1, rw_a2, rw_g1, rw_g2, rw_k_k, rw_k_a, rw_r_k, rw_gn_g, rw_gn_b, kv_norm, w_k, w_v, b_w_q, b_w_o, b_lambda, b_subln, final_norm):
    bp, tp, d = x_prompt.shape
    bd, ts, _ = x_sample.shape
    past = cache_k.shape[1] - N_META
    n_small = N_META + bd * ts
    vec = lambda a: a.reshape(1, -1).astype(f32)
    wb = lambda a: a.astype(bf16)

    rw = dict(norm_mix=vec(norm_mix[0]), mu=rw_mu[0], w_r=wb(rw_w_r[0]), w_k=wb(rw_w_k[0]), w_v=wb(rw_w_v[0]),
              w_o=wb(rw_w_o[0]), w0=vec(rw_w0[0]), w1=wb(rw_w1[0]), w2=wb(rw_w2[0]), a0=vec(rw_a0[0]),
              a1=wb(rw_a1[0]), a2=wb(rw_a2[0]), g1=wb(rw_g1[0]), g2=wb(rw_g2[0]), k_k=vec(rw_k_k[0]),
              k_a=vec(rw_k_a[0]), r_k=vec(rw_r_k[0]), gn_g=vec(rw_gn_g[0]), gn_b=vec(rw_gn_b[0]))
    wg_all, wu_all, wd_all = wb(ffn_w_gate), wb(ffn_w_up), wb(ffn_w_down)
    ffn_half = lambda x, li, j, **kw: _ffn(x, vec(norm_ffn[li, j]), wg_all, wu_all, wd_all, li, j, **kw)
    wk_b, wv_b, wq_b, wo_b = wb(w_k), wb(w_v), wb(b_w_q[0]), wb(b_w_o[0])
    lp = b_lambda[0].astype(f32)
    sub = vec(b_subln[0])
    fin = vec(final_norm)

    seq = N_META
    xs = jnp.concatenate([meta_tokens.astype(f32), x_sample.reshape(bd * ts, d)], axis=0)
    n_seq_s = n_small // seq
    shift_s0 = jnp.concatenate([jnp.zeros((1, d), f32), state_shift[:, 0]], axis=0)[:, None, :]
    wkv_s0 = jnp.concatenate([jnp.zeros((1,) + state_wkv.shape[2:], f32), state_wkv[:, 0]], axis=0)
    pos_s = np.concatenate([np.arange(N_META), np.tile(N_META + past + np.arange(ts), bd)])
    tabs_s = _rope_tables(pos_s)

    (x1s,) = ffn_half(xs, 0, 0, tm=n_small)
    r, lw, k, v, a_s, b_s, g, hl_s = _rwkv_proj(x1s, shift_s0, rw, tm=seq, tiles_per_seq=1)
    sq = lambda z: z.reshape(n_seq_s, seq, d)
    z, st_s = _wkv(sq(r), sq(lw), sq(k), sq(v), sq(a_s), sq(b_s), sq(g), rw, wkv_s0, L=seq, nb=WKV_SMALL_SEQS_PER_STEP)
    x3s, k_s, v_s, kb_s, vb_s = ffn_half(x1s, 0, 1, tm=n_small, attn=(z.reshape(n_small, d), rw["w_o"]),
                                         kv=(vec(kv_norm), wk_b, wv_b, tabs_s, 1, None))
    x4s, q_s = ffn_half(x3s, 1, 0, tm=n_small, q=(vec(norm_mix[1]), wq_b, tabs_s, 1))
    o_meta = _attn_full(q_s[None, :N_META], kb_s[None, :N_META], vb_s[None, :N_META], lp, sub)
    bs3 = lambda z: z[N_META:].reshape(bd, ts, d)
    cache_rows = lambda z: z.reshape(bd, (N_META + past) * N_HEADS, V_DIM)
    o_samp = _attn_full(bs3(q_s), bs3(kb_s), bs3(vb_s), lp, sub, cache=(cache_rows(cache_k), cache_rows(cache_v)))
    o_s = jnp.concatenate([o_meta[0], o_samp.reshape(bd * ts, d)], axis=0)
    (ys,) = ffn_half(x4s, 1, 1, tm=n_small, attn=(o_s, wo_b), final_g=fin)

    n_f = bp * tp
    xf = x_prompt.reshape(n_f, d)
    tabs_f = _rope_tables(N_META + np.arange(tp))
    shift_f0 = jnp.broadcast_to(hl_s[0:1], (bp, 1, d))
    wkv_f0 = jnp.broadcast_to(st_s[0:1], (bp,) + st_s.shape[1:])

    (x1,) = ffn_half(xf, 0, 0, tm=FRAME_TILE)
    r, lw, k, v, a_s, b_s, g, hl_f = _rwkv_proj(x1, shift_f0, rw, tm=RWKV_TILE, tiles_per_seq=tp // RWKV_TILE)
    sq = lambda z: z.reshape(bp, tp, d)
    z, st_f = _wkv(sq(r), sq(lw), sq(k), sq(v), sq(a_s), sq(b_s), sq(g), rw, wkv_f0, L=WKV_CHUNK, nb=WKV_SEQS_PER_STEP)
    tab_blocks = tp // FRAME_TILE
    x3, k_f, v_f, kb_f, vb_f = ffn_half(
        x1, 0, 1, tm=FRAME_TILE, attn=(z.reshape(n_f, d), rw["w_o"]),
        kv=(vec(kv_norm), wk_b, wv_b, tabs_f, tab_blocks, (bp, N_META + tp, N_META)))
    x4, q_f = ffn_half(x3, 1, 0, tm=FRAME_TILE, q=(vec(norm_mix[1]), wq_b, tabs_f, tab_blocks))
    o_f = _attn_prompt(sq(q_f), sq(kb_f), sq(vb_f), kb_s[:N_META], vb_s[:N_META], lp, sub, tq=ATTN_Q_TILE,
                       hp=ATTN_HEADS_PER_STEP)
    (yf,) = ffn_half(x4, 1, 1, tm=FRAME_TILE, attn=(o_f.reshape(n_f, d), wo_b), final_g=fin)

    y_prompt = yf.reshape(bp, tp, d)
    y_sample = ys[N_META:].reshape(bd, ts, d)
    wkv_p = st_f[:, None]
    wkv_s = st_s[1:][:, None]
    last_tile = tp // RWKV_TILE - 1
    shift_p = hl_f.reshape(bp, tp // RWKV_TILE, d)[:, last_tile][:, None]
    shift_s = hl_s[1:, 0][:, None]
    meta_rows = lambda z: z[:N_META].reshape(N_META * N_HEADS, V_DIM)
    new_k_p = _put_rows(k_f, meta_rows(k_s)).reshape(bp, N_META + tp, N_HEADS, V_DIM)
    new_v_p = _put_rows(v_f, meta_rows(v_s)).reshape(bp, N_META + tp, N_HEADS, V_DIM)
    new_k_s = k_s[N_META:].reshape(bd, ts, N_HEADS, V_DIM)
    new_v_s = v_s[N_META:].reshape(bd, ts, N_HEADS, V_DIM)
    return (y_prompt, y_sample, wkv_p, shift_p, new_k_p, new_v_p, wkv_s, shift_s, new_k_s, new_v_s)
```

```python
import functools
import math

import jax
import jax.numpy as jnp
import numpy as np
from jax import lax
from jax.experimental import pallas as pl
from jax.experimental.pallas import tpu as pltpu

f32 = jnp.float32
bf16 = jnp.bfloat16

D_MODEL = 1024
CHUNK = 64
N_META = 16
RW_HEAD = 64
RW_HEADS = D_MODEL // RW_HEAD
GN_EPS = 64e-5
HEAD_DIM = 64
N_HEADS = D_MODEL // (2 * HEAD_DIM)
V_DIM = 2 * HEAD_DIM
ROPE_DIM = HEAD_DIM // 4
ROPE_THETA = 500000.0
SUBLN_EPS = 1e-5
NEG_INF = -1e30
NORM_EPS = 1e-6
LAM_INIT = 0.8 - 0.6 * math.exp(-0.3 * 1)
LOG2E = math.log2(math.e)

LANES = 128
SUBLANES = 8
HEADS_PER_VREG = LANES // RW_HEAD
N_PAIRS = D_MODEL // LANES
VMEM_LIMIT = 56 * 1024 * 1024


def _cparams(sem):
    return pltpu.CompilerParams(dimension_semantics=sem, vmem_limit_bytes=VMEM_LIMIT)


def _dot(a, b):
    return jnp.dot(a, b, preferred_element_type=f32)


def _dot_nt(a, b):
    return lax.dot_general(a, b, (((1,), (1,)), ((), ())), preferred_element_type=f32)


def _dot_tn(a, b):
    return lax.dot_general(a, b, (((0,), (0,)), ((), ())), preferred_element_type=f32)


def _split3(x):
    h1 = x.astype(bf16)
    r1 = x - h1.astype(f32)
    h2 = r1.astype(bf16)
    h3 = (r1 - h2.astype(f32)).astype(bf16)
    return h1, h2, h3


def _sigmoid(x):
    return 1.0 / (1.0 + jnp.exp(-x))


def _rms(x, g, eps):
    return x * lax.rsqrt(jnp.mean(x * x, axis=-1, keepdims=True) + eps) * g


def _headsum_lanes(x):
    first = lax.broadcasted_iota(jnp.int32, (x.shape[0], LANES), 1) < RW_HEAD
    out = []
    for j in range(x.shape[1] // LANES):
        xs = x[:, j * LANES:(j + 1) * LANES]
        both = jnp.sum(xs, axis=-1, keepdims=True)
        head_a = jnp.sum(jnp.where(first, xs, 0.0), axis=-1, keepdims=True)
        out.append(jnp.where(first, head_a, both - head_a))
    return jnp.concatenate(out, axis=1)


def _rope(x, cos, sin_a, sin_b):
    reps = x.shape[1] // LANES
    c = jnp.concatenate([cos] * reps, axis=1)
    sa = jnp.concatenate([sin_a] * reps, axis=1)
    sb = jnp.concatenate([sin_b] * reps, axis=1)
    half = ROPE_DIM // 2
    return x * c + pltpu.roll(x, x.shape[1] - half, 1) * sa + pltpu.roll(x, half, 1) * sb


def _row(i):
    return (i, 0)


def _fixed2(i):
    return (0, 0)


def _rows(tm, w):
    return pl.BlockSpec((tm, w), _row)


def _whole(shape):
    return pl.BlockSpec(shape, _fixed2, pipeline_mode=pl.Buffered(1))


def _ffn_kernel(*refs, attn_in, final_norm, emit_q, emit_kv, cache_layout):
    it = iter(refs)
    x_ref, g_ref, wg_ref, wu_ref, wd_ref = (next(it) for _ in range(5))
    x = x_ref[...]
    if attn_in:
        o_in_ref, wo_ref = next(it), next(it)
        x = x + _dot(o_in_ref[...], wo_ref[...])
    xb = _rms(x, g_ref[...], NORM_EPS).astype(bf16)
    gate = _dot(xb, wg_ref[...])
    up = _dot(xb, wu_ref[...])
    act = (gate * _sigmoid(gate) * up).astype(bf16)
    y = x + 0.5 * _dot(act, wd_ref[...])
    if final_norm:
        y = _rms(y, next(it)[...], NORM_EPS)
    if emit_q:
        nq_ref, wq_ref, cos_ref, sa_ref, sb_ref = (next(it) for _ in range(5))
    if emit_kv:
        nkv_ref, wk_ref, wv_ref, cos_ref, sa_ref, sb_ref = (next(it) for _ in range(6))
    next(it)[...] = y
    if emit_q:
        hb = _rms(y, nq_ref[...], NORM_EPS).astype(bf16)
        q = _rope(_dot(hb, wq_ref[...]), cos_ref[...], sa_ref[...], sb_ref[...])
        next(it)[...] = (q * (HEAD_DIM ** -0.5 * LOG2E)).astype(bf16)
    if emit_kv:
        k_ref, v_ref, kb_ref, vb_ref = (next(it) for _ in range(4))
        hb = _rms(y, nkv_ref[...], NORM_EPS).astype(bf16)
        k = _rope(_dot(hb, wk_ref[...]), cos_ref[...], sa_ref[...], sb_ref[...])
        v = _dot(hb, wv_ref[...])
        kb_ref[...] = k.astype(bf16)
        vb_ref[...] = v.astype(bf16)
        if cache_layout:
            tm = k.shape[0]
            for h in range(N_HEADS):
                k_ref[0, pl.ds(h, tm, stride=N_HEADS), :] = k[:, h * V_DIM:(h + 1) * V_DIM]
                v_ref[0, pl.ds(h, tm, stride=N_HEADS), :] = v[:, h * V_DIM:(h + 1) * V_DIM]
        else:
            k_ref[...] = k
            v_ref[...] = v


def _ffn(x, g, wg, wu, wd, layer, half, tm, attn=None, final_g=None, q=None, kv=None):
    n, d = x.shape
    dff = wg.shape[-1]
    row = _rows(tm, d)

    def pick(rows, cols):
        return pl.BlockSpec((None, None, rows, cols), lambda i: (layer, half, 0, 0), pipeline_mode=pl.Buffered(1))

    ins = [x, g, wg, wu, wd]
    specs = [row, _whole((1, d)), pick(d, dff), pick(d, dff), pick(dff, d)]
    outs = [jax.ShapeDtypeStruct((n, d), f32)]
    out_specs = [row]
    if attn is not None:
        ins += list(attn)
        specs += [row, _whole((d, d))]
    if final_g is not None:
        ins.append(final_g)
        specs.append(_whole((1, d)))
    cache_rows = None
    if q is not None or kv is not None:
        *weights, tabs, tab_blocks = (q if q is not None else kv[:-1])
        tab = pl.BlockSpec((tm, LANES), lambda i: (i % tab_blocks, 0))
        ins += list(weights) + list(tabs)
        specs += [_whole((1, d))] + [_whole((d, d))] * (len(weights) - 1) + [tab] * 3
    if q is not None:
        outs.append(jax.ShapeDtypeStruct((n, d), bf16))
        out_specs.append(row)
    if kv is not None:
        cache_rows = kv[-1]
        if cache_rows is None:
            f32_spec, f32_shape = row, jax.ShapeDtypeStruct((n, d), f32)
        else:
            n_seq, rows_per_seq, first_row = cache_rows
            tiles_per_seq = n // (n_seq * tm)
            f32_spec = pl.BlockSpec(
                (pl.Element(1), pl.Element(tm * N_HEADS), pl.Element(V_DIM)),
                lambda i: (i // tiles_per_seq, (first_row + (i % tiles_per_seq) * tm) * N_HEADS, 0))
            f32_shape = jax.ShapeDtypeStruct((n_seq, rows_per_seq * N_HEADS, V_DIM), f32)
        outs += [f32_shape] * 2 + [jax.ShapeDtypeStruct((n, d), bf16)] * 2
        out_specs += [f32_spec, f32_spec, row, row]
    return pl.pallas_call(
        functools.partial(_ffn_kernel, attn_in=attn is not None, final_norm=final_g is not None,
                          emit_q=q is not None, emit_kv=kv is not None, cache_layout=cache_rows is not None),
        grid=(n // tm,),
        in_specs=specs,
        out_specs=out_specs,
        out_shape=outs,
        compiler_params=_cparams(("parallel",)),
        name="ffn_half",
    )(*ins)


def _rwkv_proj_kernel(x_ref, xprev_ref, shift_ref, nm_ref, mu_ref, wr_ref, wk_ref, wv_ref,
                      w0_ref, w1_ref, w2_ref, a0_ref, a1_ref, a2_ref, g1_ref, g2_ref,
                      kk_ref, ka_ref,
                      r_ref, lw_ref, k_ref, v_ref, as_ref, bs_ref, g_ref, hlast_ref,
                      h_scr, *, tm, tiles_per_seq):
    nm = nm_ref[...]
    h = _rms(x_ref[...], nm, NORM_EPS)
    hlast_ref[0] = h[tm - 1:tm, :]
    first = (pl.program_id(0) % tiles_per_seq) == 0
    last = SUBLANES - 1
    h_before = _rms(xprev_ref[last:SUBLANES, :], nm, NORM_EPS)
    h_scr[SUBLANES:tm + SUBLANES, :] = h
    h_scr[last:SUBLANES, :] = jnp.where(first, shift_ref[0], h_before)
    xx = h_scr[last:tm + last, :] - h
    mu = mu_ref[...]

    def mix(m):
        return (h + xx * mu[m:m + 1, :]).astype(bf16)

    r = _dot(mix(0), wr_ref[...])
    k = _dot(mix(2), wk_ref[...])
    v = _dot(mix(3), wv_ref[...])
    ww = w0_ref[...] + _dot(jnp.tanh(_dot(mix(1), w1_ref[...])).astype(bf16), w2_ref[...])
    lw = -math.exp(-0.5) * _sigmoid(ww)
    a = _sigmoid(a0_ref[...] + _dot(_dot(mix(4), a1_ref[...]).astype(bf16), a2_ref[...]))
    g = _dot(_sigmoid(_dot(mix(5), g1_ref[...])).astype(bf16), g2_ref[...])
    kk = k * kk_ref[...]
    norm = jnp.sqrt(_headsum_lanes(kk * kk))
    kk = kk / jnp.maximum(norm, 1e-12)
    r_ref[...] = r
    lw_ref[...] = lw
    k_ref[...] = k * (1.0 + (a - 1.0) * ka_ref[...])
    v_ref[...] = v
    as_ref[...] = -kk
    bs_ref[...] = kk * a
    g_ref[...] = g


def _rwkv_proj(x, shift0, p, tm, tiles_per_seq):
    n, d = x.shape
    nt = n // tm
    row = _rows(tm, d)
    lw_, la_, lg_ = p["w1"].shape[1], p["a1"].shape[1], p["g1"].shape[1]
    specs = [
        row,
        pl.BlockSpec((SUBLANES, d), lambda i: (jnp.maximum(i * (tm // SUBLANES) - 1, 0), 0)),
        pl.BlockSpec((1, 1, d), lambda i: (i // tiles_per_seq, 0, 0)),
        _whole((1, d)), _whole((6, d)),
        _whole((d, d)), _whole((d, d)), _whole((d, d)),
        _whole((1, d)), _whole((d, lw_)), _whole((lw_, d)),
        _whole((1, d)), _whole((d, la_)), _whole((la_, d)),
        _whole((d, lg_)), _whole((lg_, d)),
        _whole((1, d)), _whole((1, d)),
    ]
    outs = [jax.ShapeDtypeStruct((n, d), f32)] * 7 + [jax.ShapeDtypeStruct((nt, 1, d), f32)]
    out_specs = [row] * 7 + [pl.BlockSpec((1, 1, d), lambda i: (i, 0, 0))]
    return pl.pallas_call(
        functools.partial(_rwkv_proj_kernel, tm=tm, tiles_per_seq=tiles_per_seq),
        grid=(nt,),
        in_specs=specs,
        out_specs=out_specs,
        out_shape=outs,
        scratch_shapes=[pltpu.VMEM((tm + SUBLANES, d), f32)],
        compiler_params=_cparams(("parallel",)),
        name="rwkv_proj",
    )(x, x, shift0, p["norm_mix"], p["mu"], p["w_r"], p["w_k"], p["w_v"],
      p["w0"], p["w1"], p["w2"], p["a0"], p["a1"], p["a2"], p["g1"], p["g2"],
      p["k_k"], p["k_a"])


def _wkv_kernel(r_ref, lw_ref, k_ref, v_ref, a_ref, b_ref, g_ref, rk_ref, gg_ref, gb_ref, s0_ref,
                z_ref, sT_ref, s_scr, *, L, nb):
    c = pl.program_id(1)
    seqs = range(nb)
    units = range(nb * N_PAIRS)

    @pl.when(c == 0)
    def _():
        zero = jnp.zeros((RW_HEAD, RW_HEAD), f32)
        for u in units:
            b, p = divmod(u, N_PAIRS)
            top = jnp.concatenate([s0_ref[b, HEADS_PER_VREG * p], zero], axis=1)
            bottom = jnp.concatenate([zero, s0_ref[b, HEADS_PER_VREG * p + 1]], axis=1)
            s_scr[u] = jnp.concatenate([top, bottom], axis=0).T

    L2 = 2 * L
    ri = lax.broadcasted_iota(jnp.int32, (L, L), 0)
    ci = lax.broadcasted_iota(jnp.int32, (L, L), 1)
    cum = jnp.where(ri >= ci, 1.0, 0.0).astype(bf16)
    rs = lax.broadcasted_iota(jnp.int32, (L2, LANES), 0)
    ls = lax.broadcasted_iota(jnp.int32, (L2, LANES), 1)
    own = (rs < L) == (ls < RW_HEAD)
    rg = lax.broadcasted_iota(jnp.int32, (L2, L2), 0) & (L - 1)
    cg = lax.broadcasted_iota(jnp.int32, (L2, L2), 1) & (L - 1)
    strict = rg > cg
    incl = rg >= cg

    lw = [lw_ref[b] for b in seqs]
    parts = [_split3(lw[b]) for b in seqs]
    cs = [_dot(cum, parts[b][0]) + _dot(cum, parts[b][1]) + _dot(cum, parts[b][2]) for b in seqs]
    w_in = [jnp.exp(cs[b]) for b in seqs]
    w_inv = [jnp.exp(-cs[b]) for b in seqs]
    pairs = units

    def stack(xs):
        out = []
        for u in units:
            b, p = divmod(u, N_PAIRS)
            xp = xs[b][:, p * LANES:(p + 1) * LANES]
            out.append(jnp.where(own, jnp.concatenate([xp, xp], axis=0), 0.0).astype(bf16))
        return out

    a_st = stack([a_ref[b] * jnp.exp(cs[b] - lw[b]) for b in seqs])
    r_st = stack([r_ref[b] * w_in[b] for b in seqs])
    b_st = stack([b_ref[b] * w_inv[b] for b in seqs])
    k_st = stack([k_ref[b] * w_inv[b] for b in seqs])
    v_st = stack([v_ref[b] for b in seqs])
    gram = [_dot_nt(jnp.concatenate([a_st[p], r_st[p]], axis=0), jnp.concatenate([b_st[p], k_st[p]], axis=0))
            for p in pairs]
    m = [jnp.where(strict, gram[p][:L2, :L2], 0.0) for p in pairs]
    a_ak = [jnp.where(strict, gram[p][:L2, L2:], 0.0).astype(bf16) for p in pairs]
    a_rb = [jnp.where(incl, gram[p][L2:, :L2], 0.0).astype(bf16) for p in pairs]
    a_rk = [jnp.where(incl, gram[p][L2:, L2:], 0.0).astype(bf16) for p in pairs]
    s = [s_scr[p] for p in pairs]
    sb = [s[p].astype(bf16) for p in pairs]

    rhs = [_dot(jnp.concatenate([a_st[p], a_ak[p]], axis=1), jnp.concatenate([sb[p], v_st[p]], axis=0))
           for p in pairs]
    t = m
    mb = [m[p].astype(bf16) for p in pairs]
    m = [_dot(mb[p], mb[p]) for p in pairs]
    n_sq = L.bit_length() - 2
    for k in range(1, n_sq + 1):
        mb = [m[p].astype(bf16) for p in pairs]
        if k < n_sq:
            both = [_dot(mb[p], jnp.concatenate([t[p].astype(bf16), mb[p]], axis=1)) for p in pairs]
            t = [t[p] + m[p] + both[p][:, :L2] for p in pairs]
            m = [both[p][:, L2:] for p in pairs]
        else:
            t = [t[p] + m[p] + _dot(mb[p], t[p].astype(bf16)) for p in pairs]
    ub = [(rhs[p] + _dot(t[p].astype(bf16), rhs[p].astype(bf16))).astype(bf16) for p in pairs]
    y_st = [_dot(jnp.concatenate([r_st[p], a_rb[p], a_rk[p]], axis=1),
                 jnp.concatenate([sb[p], ub[p], v_st[p]], axis=0)) for p in pairs]
    for b in seqs:
        y = jnp.concatenate([y_st[u][:L] + y_st[u][L:] for u in units[b * N_PAIRS:(b + 1) * N_PAIRS]], axis=1)
        d = y - _headsum_lanes(y) * (1.0 / RW_HEAD)
        var = _headsum_lanes(d * d) * (1.0 / RW_HEAD)
        yn = d * lax.rsqrt(var + GN_EPS) * gg_ref[...] + gb_ref[...]
        bonus = _headsum_lanes(r_ref[b] * k_ref[b] * rk_ref[...]) * v_ref[b]
        z_ref[b] = ((yn + bonus) * g_ref[b]).astype(z_ref.dtype)
    eye = (lax.broadcasted_iota(jnp.int32, (LANES, LANES), 0)
           == lax.broadcasted_iota(jnp.int32, (LANES, LANES), 1)).astype(f32)
    for u in units:
        b, p = divmod(u, N_PAIRS)
        w_rows = jnp.sum(eye * w_in[b][L - 1:L, p * LANES:(p + 1) * LANES], axis=1, keepdims=True)
        s_scr[u] = (s[u] + _dot_tn(jnp.concatenate([b_st[u], k_st[u]], axis=0),
                                   jnp.concatenate([ub[u], v_st[u]], axis=0))) * w_rows

    @pl.when(c == pl.num_programs(1) - 1)
    def _():
        for u in units:
            b, p = divmod(u, N_PAIRS)
            s_vk = s_scr[u].T
            sT_ref[b, HEADS_PER_VREG * p] = s_vk[:RW_HEAD, :RW_HEAD]
            sT_ref[b, HEADS_PER_VREG * p + 1] = s_vk[RW_HEAD:, RW_HEAD:]


def _wkv(r, lw, k, v, a, b, g, p, s0, L, nb):
    nseq, t, d = r.shape
    blk = pl.BlockSpec((nb, L, d), lambda s, c: (s, c, 0))
    st = pl.BlockSpec((nb, RW_HEADS, RW_HEAD, RW_HEAD), lambda s, c: (s, 0, 0, 0))
    vec = pl.BlockSpec((1, d), lambda s, c: (0, 0))
    return pl.pallas_call(
        functools.partial(_wkv_kernel, L=L, nb=nb),
        grid=(nseq // nb, t // L),
        in_specs=[blk] * 7 + [vec] * 3 + [st],
        out_specs=[blk, st],
        out_shape=[jax.ShapeDtypeStruct((nseq, t, d), bf16),
                   jax.ShapeDtypeStruct((nseq, RW_HEADS, RW_HEAD, RW_HEAD), f32)],
        scratch_shapes=[pltpu.VMEM((nb * N_PAIRS, LANES, LANES), f32)],
        compiler_params=_cparams(("parallel", "arbitrary")),
        name="wkv_scan",
    )(r, lw, k, v, a, b, g, p["r_k"], p["gn_g"], p["gn_b"], s0)


def _put_rows_kernel(big_ref, rows_ref, o_ref):
    del big_ref
    o_ref[0] = rows_ref[...]


def _put_rows(big, rows):
    n_seq = big.shape[0]
    r = rows.shape[0]
    return pl.pallas_call(
        _put_rows_kernel,
        grid=(n_seq,),
        in_specs=[pl.BlockSpec(memory_space=pl.ANY), pl.BlockSpec((r, LANES), lambda s: (0, 0))],
        out_specs=pl.BlockSpec((1, r, LANES), lambda s: (s, 0, 0)),
        out_shape=jax.ShapeDtypeStruct(big.shape, big.dtype),
        input_output_aliases={0: 0},
        compiler_params=_cparams(("parallel",)),
        name="put_meta_rows",
    )(big, rows)


def _lam(lp):
    e1 = jnp.exp(jnp.sum(lp[0:1, :] * lp[1:2, :], axis=-1, keepdims=True))
    e2 = jnp.exp(jnp.sum(lp[2:3, :] * lp[3:4, :], axis=-1, keepdims=True))
    return e1 - e2 + LAM_INIT


def _split_heads(q):
    lane = lax.broadcasted_iota(jnp.int32, q.shape, 1)
    zero = jnp.zeros_like(q)
    return jnp.where(lane < HEAD_DIM, q, zero), jnp.where(lane >= HEAD_DIM, q, zero)


def _stack_components(q):
    q1, q2 = _split_heads(q)
    return jnp.concatenate([q1, q2], axis=0)


def _with_ones(v):
    return jnp.concatenate([v, jnp.ones(v.shape, v.dtype)], axis=1)


def _attn_finish(acc, lp, sub):
    n = acc.shape[0] // 2
    o = acc[:n, :V_DIM] / acc[:n, V_DIM:] - _lam(lp) * (acc[n:, :V_DIM] / acc[n:, V_DIM:])
    o = o * lax.rsqrt(jnp.mean(o * o, axis=-1, keepdims=True) + SUBLN_EPS) * sub
    return o * (1.0 - LAM_INIT)


def _attn_prompt_kernel(q_ref, k_ref, v_ref, km_ref, vm_ref, bias_ref, lp_ref, sub_ref, *rest, tq, hp, n_tiles):
    o_ref = rest[-1]
    head_cols = [slice(h * V_DIM, (h + 1) * V_DIM) for h in range(hp)]
    qs, head_of = [], []
    for h in range(hp):
        qs += list(_split_heads(q_ref[0, :, head_cols[h]]))
        head_of += [h, h]
    chains = range(len(qs))

    half = tq // 2

    def spans(diagonal):
        return [(0, half, half), (half, tq, tq)] if diagonal else [(0, tq, tq)]

    def scores(j, diagonal):
        kj = [k_ref[0, j * tq:(j + 1) * tq, head_cols[h]] for h in range(hp)]
        return [_dot_nt(qs[c][r0:r1], kj[head_of[c]][:nk]) for c in chains for r0, r1, nk in spans(diagonal)]

    def tile(j, s, carry, diagonal):
        sp = spans(diagonal)
        units = [(c, r0, r1, nk) for c in chains for r0, r1, nk in sp]
        idx = range(len(units))
        vj = [_with_ones(v_ref[0, j * tq:(j + 1) * tq, head_cols[h]]) for h in range(hp)]
        m_old = [carry[2 * c][r0:r1] for c, r0, r1, _ in units]
        acc_old = [carry[2 * c + 1][r0:r1] for c, r0, r1, _ in units]
        if diagonal:
            bias = bias_ref[...]
            s = [s[u] + bias[units[u][1]:units[u][2], :units[u][3]] for u in idx]
            sm = [_dot_nt(qs[c][r0:r1], km_ref[:, head_cols[head_of[c]]]) for c, r0, r1, _ in units]
        m_blk = [jnp.max(s[u], axis=-1, keepdims=True) for u in idx]
        if diagonal:
            m_blk = [jnp.maximum(m_blk[u], jnp.max(sm[u], axis=-1, keepdims=True)) for u in idx]
        m_new = [jnp.maximum(m_old[u], m_blk[u]) for u in idx]
        alpha = [jnp.exp2(m_old[u] - m_new[u]) for u in idx]
        p = [jnp.exp2((s[u] - m_new[u]).astype(bf16)) for u in idx]
        acc = [alpha[u] * acc_old[u] + _dot(p[u], vj[head_of[units[u][0]]][:units[u][3]]) for u in idx]
        if diagonal:
            pm = [jnp.exp2((sm[u] - m_new[u]).astype(bf16)) for u in idx]
            vm = [_with_ones(vm_ref[:, head_cols[h]]) for h in range(hp)]
            acc = [acc[u] + _dot(pm[u], vm[head_of[units[u][0]]]) for u in idx]
        out = []
        for c in chains:
            mine = [u for u in idx if units[u][0] == c]
            out += [jnp.concatenate([m_new[u] for u in mine], axis=0) if len(mine) > 1 else m_new[mine[0]],
                    jnp.concatenate([acc[u] for u in mine], axis=0) if len(mine) > 1 else acc[mine[0]]]
        return tuple(out)

    init = []
    for _ in chains:
        init += [jnp.full((tq, 1), NEG_INF, f32), jnp.zeros((tq, 2 * V_DIM), f32)]
    carry = tuple(init)
    s_next = scores(0, n_tiles == 1)
    for j in range(n_tiles):
        s_cur = s_next
        if j + 1 < n_tiles:
            s_next = scores(j + 1, j + 2 == n_tiles)
        carry = tile(j, s_cur, carry, j == n_tiles - 1)
    for h in range(hp):
        acc = jnp.concatenate([carry[2 * c + 1] for c in (2 * h, 2 * h + 1)], axis=0)
        o_ref[0, :, head_cols[h]] = _attn_finish(acc, lp_ref[...], sub_ref[...]).astype(o_ref.dtype)


def _attn_prompt(q, k, v, km, vm, lp, sub, tq, hp):
    b, t, d = q.shape
    w = hp * V_DIM
    mspec = pl.BlockSpec((N_META, w), lambda bi, h: (0, h))
    chunk_of = jnp.arange(tq) // CHUNK
    bias = jnp.where(chunk_of[None, :] <= chunk_of[:, None], 0.0, NEG_INF).astype(f32)
    out = None
    for i in range(t // tq):
        qspec = pl.BlockSpec((1, tq, w), lambda bi, h, i=i: (bi, i, h))
        kspec = pl.BlockSpec((1, (i + 1) * tq, w), lambda bi, h: (bi, 0, h))
        ins = [q, k, v, km, vm, bias, lp, sub]
        specs = [qspec, kspec, kspec, mspec, mspec,
                 pl.BlockSpec((tq, tq), lambda bi, h: (0, 0), pipeline_mode=pl.Buffered(1)),
                 pl.BlockSpec((4, HEAD_DIM), lambda bi, h: (0, 0)),
                 pl.BlockSpec((1, V_DIM), lambda bi, h: (0, 0))]
        if out is not None:
            ins.append(out)
            specs.append(pl.BlockSpec(memory_space=pl.ANY))
        out = pl.pallas_call(
            functools.partial(_attn_prompt_kernel, tq=tq, hp=hp, n_tiles=i + 1),
            grid=(b, N_HEADS // hp),
            in_specs=specs,
            out_specs=qspec,
            out_shape=jax.ShapeDtypeStruct((b, t, d), bf16),
            input_output_aliases={} if i == 0 else {len(ins) - 1: 0},
            compiler_params=_cparams(("parallel", "parallel")),
            name=f"attn_prompt_{i}",
        )(*ins)
    return out


def _attn_full_kernel(*refs, n_cache):
    if n_cache:
        q_ref, kc_ref, vc_ref, kn_ref, vn_ref, lp_ref, sub_ref, o_ref = refs
    else:
        q_ref, kn_ref, vn_ref, lp_ref, sub_ref, o_ref = refs
    lp = lp_ref[...]
    sub = sub_ref[...]
    for h in range(N_HEADS):
        cols = slice(h * V_DIM, (h + 1) * V_DIM)
        qq = _stack_components(q_ref[0, :, cols])
        keys = [kn_ref[0, :, cols]]
        vals = [vn_ref[0, :, cols]]
        if n_cache:
            for first, count in ((0, N_META), (N_META, n_cache - N_META)):
                rows = pl.ds(first * N_HEADS + h, count, stride=N_HEADS)
                keys.append(kc_ref[0, rows, :].astype(bf16))
                vals.append(vc_ref[0, rows, :].astype(bf16))
        scores = [_dot_nt(qq, kk) for kk in keys]
        m = functools.reduce(jnp.maximum, [jnp.max(s, axis=-1, keepdims=True) for s in scores])
        acc = sum(_dot(jnp.exp2(s - m).astype(bf16), _with_ones(vv)) for s, vv in zip(scores, vals))
        o_ref[0, :, cols] = _attn_finish(acc, lp, sub).astype(o_ref.dtype)


def _attn_full(q, kn, vn, lp, sub, cache=None):
    b, tq, d = q.shape
    spec = pl.BlockSpec((1, tq, d), lambda bi: (bi, 0, 0))
    ins, specs, n_cache = [q], [spec], 0
    if cache is not None:
        n_cache = cache[0].shape[1] // N_HEADS
        cspec = pl.BlockSpec((1, n_cache * N_HEADS, V_DIM), lambda bi: (bi, 0, 0))
        ins += list(cache)
        specs += [cspec, cspec]
    ins += [kn, vn, lp, sub]
    specs += [spec, spec, pl.BlockSpec((4, HEAD_DIM), lambda bi: (0, 0)),
              pl.BlockSpec((1, V_DIM), lambda bi: (0, 0))]
    return pl.pallas_call(
        functools.partial(_attn_full_kernel, n_cache=n_cache),
        grid=(b,),
        in_specs=specs,
        out_specs=spec,
        out_shape=jax.ShapeDtypeStruct((b, tq, d), bf16),
        compiler_params=_cparams(("parallel",)),
        name="attn_full",
    )(*ins)


FRAME_TILE = 512
RWKV_TILE = 256
ATTN_Q_TILE = 512
ATTN_HEADS_PER_STEP = 2
WKV_CHUNK = 64
WKV_SMALL_SEQS_PER_STEP = 3
WKV_SEQS_PER_STEP = 2


def _rope_tables(pos):
    half = ROPE_DIM // 2
    inv = np.power(np.float64(ROPE_THETA), -np.arange(0, ROPE_DIM, 2, dtype=np.float64) / ROPE_DIM)
    ang = pos.astype(np.float64)[:, None] * inv[None, :]
    cos, sin = np.cos(ang).astype(np.float32), np.sin(ang).astype(np.float32)
    n = pos.shape[0]
    one = np.ones((n, HEAD_DIM - ROPE_DIM), np.float32)
    zero = np.zeros((n, HEAD_DIM - ROPE_DIM), np.float32)
    zh = np.zeros((n, half), np.float32)
    c = np.concatenate([cos, cos, one], axis=1)
    sa = np.concatenate([-sin, zh, zero], axis=1)
    sb = np.concatenate([zh, sin, zero], axis=1)
    return tuple(jnp.asarray(np.concatenate([t, t], axis=1), dtype=f32) for t in (c, sa, sb))


def kernel(x_prompt, x_sample, cache_k, cache_v, state_wkv, state_shift, meta_tokens, norm_ffn, norm_mix, ffn_w_gate, ffn_w_up, ffn_w_down, rw_mu, rw_w_r, rw_w_k, rw_w_v, rw_w_o, rw_w0, rw_w1, rw_w2, rw_a0, rw_a1, rw_a2, rw_g1, rw_g2, rw_k_k, rw_k_a, rw_r_k, rw_gn_g, rw_gn_b, kv_norm, w_k, w_v, b_w_q, b_w_o, b_lambda, b_subln, final_norm):
    bp, tp, d = x_prompt.shape
    bd, ts, _ = x_sample.shape
    past = cache_k.shape[1] - N_META
    n_small = N_META + bd * ts
    vec = lambda a: a.reshape(1, -1).astype(f32)
    wb = lambda a: a.astype(bf16)

    rw = dict(norm_mix=vec(norm_mix[0]), mu=rw_mu[0], w_r=wb(rw_w_r[0]), w_k=wb(rw_w_k[0]), w_v=wb(rw_w_v[0]),
              w_o=wb(rw_w_o[0]), w0=vec(rw_w0[0]), w1=wb(rw_w1[0]), w2=wb(rw_w2[0]), a0=vec(rw_a0[0]),
              a1=wb(rw_a1[0]), a2=wb(rw_a2[0]), g1=wb(rw_g1[0]), g2=wb(rw_g2[0]), k_k=vec(rw_k_k[0]),
              k_a=vec(rw_k_a[0]), r_k=vec(rw_r_k[0]), gn_g=vec(rw_gn_g[0]), gn_b=vec(rw_gn_b[0]))
    wg_all, wu_all, wd_all = wb(ffn_w_gate), wb(ffn_w_up), wb(ffn_w_down)
    ffn_half = lambda x, li, j, **kw: _ffn(x, vec(norm_ffn[li, j]), wg_all, wu_all, wd_all, li, j, **kw)
    wk_b, wv_b, wq_b, wo_b = wb(w_k), wb(w_v), wb(b_w_q[0]), wb(b_w_o[0])
    lp = b_lambda[0].astype(f32)
    sub = vec(b_subln[0])
    fin = vec(final_norm)

    seq = N_META
    xs = jnp.concatenate([meta_tokens.astype(f32), x_sample.reshape(bd * ts, d)], axis=0)
    n_seq_s = n_small // seq
    shift_s0 = jnp.concatenate([jnp.zeros((1, d), f32), state_shift[:, 0]], axis=0)[:, None, :]
    wkv_s0 = jnp.concatenate([jnp.zeros((1,) + state_wkv.shape[2:], f32), state_wkv[:, 0]], axis=0)
    pos_s = np.concatenate([np.arange(N_META), np.tile(N_META + past + np.arange(ts), bd)])
    tabs_s = _rope_tables(pos_s)

    (x1s,) = ffn_half(xs, 0, 0, tm=n_small)
    r, lw, k, v, a_s, b_s, g, hl_s = _rwkv_proj(x1s, shift_s0, rw, tm=seq, tiles_per_seq=1)
    sq = lambda z: z.reshape(n_seq_s, seq, d)
    z, st_s = _wkv(sq(r), sq(lw), sq(k), sq(v), sq(a_s), sq(b_s), sq(g), rw, wkv_s0, L=seq, nb=WKV_SMALL_SEQS_PER_STEP)
    x3s, k_s, v_s, kb_s, vb_s = ffn_half(x1s, 0, 1, tm=n_small, attn=(z.reshape(n_small, d), rw["w_o"]),
                                         kv=(vec(kv_norm), wk_b, wv_b, tabs_s, 1, None))
    x4s, q_s = ffn_half(x3s, 1, 0, tm=n_small, q=(vec(norm_mix[1]), wq_b, tabs_s, 1))
    o_meta = _attn_full(q_s[None, :N_META], kb_s[None, :N_META], vb_s[None, :N_META], lp, sub)
    bs3 = lambda z: z[N_META:].reshape(bd, ts, d)
    cache_rows = lambda z: z.reshape(bd, (N_META + past) * N_HEADS, V_DIM)
    o_samp = _attn_full(bs3(q_s), bs3(kb_s), bs3(vb_s), lp, sub, cache=(cache_rows(cache_k), cache_rows(cache_v)))
    o_s = jnp.concatenate([o_meta[0], o_samp.reshape(bd * ts, d)], axis=0)
    (ys,) = ffn_half(x4s, 1, 1, tm=n_small, attn=(o_s, wo_b), final_g=fin)

    n_f = bp * tp
    xf = x_prompt.reshape(n_f, d)
    tabs_f = _rope_tables(N_META + np.arange(tp))
    shift_f0 = jnp.broadcast_to(hl_s[0:1], (bp, 1, d))
    wkv_f0 = jnp.broadcast_to(st_s[0:1], (bp,) + st_s.shape[1:])

    (x1,) = ffn_half(xf, 0, 0, tm=FRAME_TILE)
    r, lw, k, v, a_s, b_s, g, hl_f = _rwkv_proj(x1, shift_f0, rw, tm=RWKV_TILE, tiles_per_seq=tp // RWKV_TILE)
    sq = lambda z: z.reshape(bp, tp, d)
    z, st_f = _wkv(sq(r), sq(lw), sq(k), sq(v), sq(a_s), sq(b_s), sq(g), rw, wkv_f0, L=WKV_CHUNK, nb=WKV_SEQS_PER_STEP)
    tab_blocks = tp // FRAME_TILE
    x3, k_f, v_f, kb_f, vb_f = ffn_half(
        x1, 0, 1, tm=FRAME_TILE, attn=(z.reshape(n_f, d), rw["w_o"]),
        kv=(vec(kv_norm), wk_b, wv_b, tabs_f, tab_blocks, (bp, N_META + tp, N_META)))
    x4, q_f = ffn_half(x3, 1, 0, tm=FRAME_TILE, q=(vec(norm_mix[1]), wq_b, tabs_f, tab_blocks))
    o_f = _attn_prompt(sq(q_f), sq(kb_f), sq(vb_f), kb_s[:N_META], vb_s[:N_META], lp, sub, tq=ATTN_Q_TILE,
                       hp=ATTN_HEADS_PER_STEP)
    (yf,) = ffn_half(x4, 1, 1, tm=FRAME_TILE, attn=(o_f.reshape(n_f, d), wo_b), final_g=fin)

    y_prompt = yf.reshape(bp, tp, d)
    y_sample = ys[N_META:].reshape(bd, ts, d)
    wkv_p = st_f[:, None]
    wkv_s = st_s[1:][:, None]
    last_tile = tp // RWKV_TILE - 1
    shift_p = hl_f.reshape(bp, tp // RWKV_TILE, d)[:, last_tile][:, None]
    shift_s = hl_s[1:, 0][:, None]
    meta_rows = lambda z: z[:N_META].reshape(N_META * N_HEADS, V_DIM)
    new_k_p = _put_rows(k_f, meta_rows(k_s)).reshape(bp, N_META + tp, N_HEADS, V_DIM)
    new_v_p = _put_rows(v_f, meta_rows(v_s)).reshape(bp, N_META + tp, N_HEADS, V_DIM)
    new_k_s = k_s[N_META:].reshape(bd, ts, N_HEADS, V_DIM)
    new_v_s = v_s[N_META:].reshape(bd, ts, N_HEADS, V_DIM)
    return (y_prompt, y_sample, wkv_p, shift_p, new_k_p, new_v_p, wkv_s, shift_s, new_k_s, new_v_s)
```
